```python
import jax, jax.numpy as jnp
from jax import lax
import numpy as np

D_MODEL = 2048
BATCH = 2
SEQ = 4096
DEPTH = 2
DEC_BATCH = 8
DEC_SEQ = 16
PAST_LEN = 1024

CHUNK = 64
N_HEADS = 16
HEAD_DIM = 128
D_INNER = N_HEADS * HEAD_DIM
Q_BLOCK = 128
EPS = 1e-6
N_MIXERS = 2
SCALE = HEAD_DIM ** -0.5

kernel_name = "fox_stickbreak_interleaved_streaming_step"


def rmsnorm(x, g):
    xf = x.astype(jnp.float32)
    y = xf * lax.rsqrt(jnp.mean(xf * xf, axis=-1, keepdims=True) + EPS)
    return (y * g.astype(jnp.float32)).astype(x.dtype)


def fox_block(q, cq, qpos, k, v, ck, kpos):
    s = jnp.einsum('bqhd,bkhd->bhqk', q, k, preferred_element_type=jnp.float32) * SCALE
    decay = jnp.transpose(cq, (0, 2, 1))[..., :, None] - jnp.transpose(ck, (0, 2, 1))[..., None, :]
    mask = kpos[None, :] <= qpos[:, None]
    p = jax.nn.softmax(jnp.where(mask, s + decay, -jnp.inf), axis=-1)
    return jnp.einsum('bhqk,bkhd->bqhd', p.astype(v.dtype), v)


def sb_block(q, qpos, k, v, kpos):
    z = jnp.einsum('bqhd,bkhd->bhqk', q, k, preferred_element_type=jnp.float32) * SCALE
    mask = kpos[None, :] < qpos[:, None]
    u = jnp.where(mask, jax.nn.log_sigmoid(-z), 0.0)
    after = lax.cumsum(u, axis=3, reverse=True) - u
    a = jnp.where(mask, jnp.exp(jax.nn.log_sigmoid(z) + after), 0.0)
    return jnp.einsum('bhqk,bkhd->bqhd', a.astype(v.dtype), v)


def sweep_query_blocks(block_fn, q_arrays):
    b, t = q_arrays[0].shape[:2]
    nb = t // Q_BLOCK
    blocks = tuple(jnp.moveaxis(a.reshape(b, nb, Q_BLOCK, *a.shape[2:]), 1, 0) for a in q_arrays)
    qpos = jnp.arange(t, dtype=jnp.int32).reshape(nb, Q_BLOCK)
    out = lax.map(lambda args: block_fn(args[0], args[1]), (blocks, qpos))
    return jnp.moveaxis(out, 0, 1).reshape(b, t, N_HEADS, HEAD_DIM)


def split_heads(a):
    return a.reshape(a.shape[0], a.shape[1], N_HEADS, HEAD_DIM)


def fox_layer(x, norm, w_in, b_f, w_out, past_k=None, past_v=None, past_logf=None):
    b, t, _ = x.shape
    proj = rmsnorm(x, norm) @ w_in
    q, k, v, gate = [proj[..., i * D_INNER:(i + 1) * D_INNER] for i in range(4)]
    q, k, v = split_heads(q), split_heads(k), split_heads(v)
    logf = jax.nn.log_sigmoid((proj[..., 4 * D_INNER:] + b_f).astype(jnp.float32))
    if past_k is None:
        c = jnp.cumsum(logf, axis=1)
        kpos = jnp.arange(t, dtype=jnp.int32)
        o = sweep_query_blocks(lambda blk, qp: fox_block(blk[0], blk[1], qp, k, v, c, kpos), (q, c))
    else:
        p_len = past_k.shape[1]
        k_all = jnp.concatenate([past_k.astype(k.dtype), k], axis=1)
        v_all = jnp.concatenate([past_v.astype(v.dtype), v], axis=1)
        c_all = jnp.cumsum(jnp.concatenate([past_logf.astype(jnp.float32), logf], axis=1), axis=1)
        qpos = p_len + jnp.arange(t, dtype=jnp.int32)
        kpos = jnp.arange(p_len + t, dtype=jnp.int32)
        o = fox_block(q, c_all[:, p_len:], qpos, k_all, v_all, c_all, kpos)
    y = o.reshape(b, t, D_INNER) * jax.nn.silu(gate)
    return x + y @ w_out, (k, v, logf)


def sb_layer(x, norm, w_in, w_out, past_k=None, past_v=None):
    b, t, _ = x.shape
    proj = rmsnorm(x, norm) @ w_in
    q, k, v, gate = [proj[..., i * D_INNER:(i + 1) * D_INNER] for i in range(4)]
    q, k, v = split_heads(q), split_heads(k), split_heads(v)
    if past_k is None:
        kpos = jnp.arange(t, dtype=jnp.int32)
        o = sweep_query_blocks(lambda blk, qp: sb_block(blk[0], qp, k, v, kpos), (q,))
    else:
        p_len = past_k.shape[1]
        k_all = jnp.concatenate([past_k.astype(k.dtype), k], axis=1)
        v_all = jnp.concatenate([past_v.astype(v.dtype), v], axis=1)
        qpos = p_len + jnp.arange(t, dtype=jnp.int32)
        kpos = jnp.arange(p_len + t, dtype=jnp.int32)
        o = sb_block(q, qpos, k_all, v_all, kpos)
    y = o.reshape(b, t, D_INNER) * jax.nn.silu(gate)
    return x + y @ w_out, (k, v)


def setup_inputs(seed: int = 0) -> dict:
    key = jax.random.key(seed)
    ks = jax.random.split(key, 16)
    f32 = jnp.float32
    n = lambda k, s: jax.random.normal(k, s, dtype=f32)
    cshape = (DEC_BATCH, PAST_LEN, N_HEADS, HEAD_DIM)
    return {
        "x_prompt": n(ks[0], (BATCH, SEQ, D_MODEL)),
        "x_sample": n(ks[1], (DEC_BATCH, DEC_SEQ, D_MODEL)),
        "cache_fox_k": n(ks[2], cshape),
        "cache_fox_v": n(ks[3], cshape),
        "cache_fox_logf": jax.nn.log_sigmoid(1.0 + n(ks[4], (DEC_BATCH, PAST_LEN, N_HEADS))),
        "cache_sb_k": n(ks[5], cshape),
        "cache_sb_v": n(ks[6], cshape),
        "norm_0": 1.0 + 0.02 * n(ks[7], (D_MODEL,)),
        "w_in_0": n(ks[8], (D_MODEL, 4 * D_INNER + N_HEADS)) * D_MODEL ** -0.5,
        "b_f_0": 1.0 + 0.1 * n(ks[9], (N_HEADS,)),
        "w_out_0": n(ks[10], (D_INNER, D_MODEL)) * D_INNER ** -0.5,
        "norm_1": 1.0 + 0.02 * n(ks[11], (D_MODEL,)),
        "w_in_1": n(ks[12], (D_MODEL, 4 * D_INNER)) * D_MODEL ** -0.5,
        "w_out_1": n(ks[13], (D_INNER, D_MODEL)) * D_INNER ** -0.5,
        "norm_f": 1.0 + 0.02 * n(ks[14], (D_MODEL,)),
    }


def reference(x_prompt, x_sample, cache_fox_k, cache_fox_v, cache_fox_logf, cache_sb_k, cache_sb_v,
              norm_0, w_in_0, b_f_0, w_out_0, norm_1, w_in_1, w_out_1, norm_f):
    xp, xs = x_prompt, x_sample
    for i in range(DEPTH):
        if i % N_MIXERS == 0:
            xp, (pk_f, pv_f, pl_f) = fox_layer(xp, norm_0, w_in_0, b_f_0, w_out_0)
            xs, (sk_f, sv_f, sl_f) = fox_layer(xs, norm_0, w_in_0, b_f_0, w_out_0,
                                               cache_fox_k, cache_fox_v, cache_fox_logf)
        else:
            xp, (pk_s, pv_s) = sb_layer(xp, norm_1, w_in_1, w_out_1)
            xs, (sk_s, sv_s) = sb_layer(xs, norm_1, w_in_1, w_out_1, cache_sb_k, cache_sb_v)
    y_prompt = rmsnorm(xp, norm_f)
    y_sample = rmsnorm(xs, norm_f)
    return (y_prompt, y_sample, pk_f, pv_f, pl_f, pk_s, pv_s, sk_f, sv_f, sl_f, sk_s, sv_s)
```

```python
from functools import partial

import jax
import jax.numpy as jnp
from jax import lax
from jax.experimental import pallas as pl
from jax.experimental.pallas import tpu as pltpu

D_MODEL = 2048
N_HEADS = 16
HEAD_DIM = 128
D_INNER = N_HEADS * HEAD_DIM
EPS = 1e-6
SCALE = HEAD_DIM ** -0.5
NEG = -1e30

LANES = 128
F32 = jnp.float32
BF16 = jnp.bfloat16

VMEM_LIMIT = 56 * 1024 * 1024

PROJ_TN = 512
ATT_T = 512
CUM_T = 512
HEAD_GROUP = 4
SAMPLE_KPAD = 128


def _params(n_axes):
    return pltpu.CompilerParams(dimension_semantics=("arbitrary",) * n_axes,
                                vmem_limit_bytes=VMEM_LIMIT)


def _log_sigmoid(z):
    return jnp.minimum(z, 0.0) - jnp.log1p(jnp.exp(-jnp.abs(z)))


def _split3(x):
    x1 = x.astype(BF16)
    r1 = x - x1.astype(F32)
    x2 = r1.astype(BF16)
    x3 = (r1 - x2.astype(F32)).astype(BF16)
    return x1, x2, x3


def _split2(x):
    x1 = x.astype(BF16)
    x2 = (x - x1.astype(F32)).astype(BF16)
    return x1, x2


def _dot(a, b):
    return jnp.dot(a, b, preferred_element_type=F32)


def _dot_nt(a, b):
    return lax.dot_general(a, b, (((1,), (1,)), ((), ())), preferred_element_type=F32)


def _proj_kernel(*refs, with_f, n_per_out):
    if with_f:
        (x_ref, nw_ref, w_ref, wf_ref, bf_ref,
         q_ref, k_ref, v_ref, g_ref, lf_ref, xn_scr) = refs
    else:
        x_ref, nw_ref, w_ref, q_ref, k_ref, v_ref, g_ref, xn_scr = refs
    n = pl.program_id(1)

    @pl.when(n == 0)
    def _():
        x = x_ref[...]
        ms = jnp.mean(x * x, axis=-1, keepdims=True)
        xn = (x * lax.rsqrt(ms + EPS)) * nw_ref[...]
        xn_scr[...] = xn.astype(BF16)
        if with_f:
            z = _dot(xn_scr[...], wf_ref[...]) + bf_ref[...]
            lf_ref[...] = _log_sigmoid(z)

    acc = _dot(xn_scr[...], w_ref[...])
    which = n // n_per_out

    @pl.when(which == 0)
    def _():
        q_ref[...] = (acc * SCALE).astype(BF16)

    @pl.when(which == 1)
    def _():
        k_ref[...] = acc

    @pl.when(which == 2)
    def _():
        v_ref[...] = acc

    @pl.when(which == 3)
    def _():
        g_ref[...] = (acc * jax.nn.sigmoid(acc)).astype(BF16)


def _in_proj(x, norm_w, w_bf, wf_bf=None, b_f=None, *, tm):
    n_tok = x.shape[0]
    with_f = wf_bf is not None
    n_per_out = D_INNER // PROJ_TN
    grid = (n_tok // tm, 4 * n_per_out)

    def out_map(j):
        return lambda m, n: (m, jnp.clip(n - j * n_per_out, 0, n_per_out - 1))

    in_specs = [pl.BlockSpec((tm, D_MODEL), lambda m, n: (m, 0)),
                pl.BlockSpec((1, D_MODEL), lambda m, n: (0, 0)),
                pl.BlockSpec((D_MODEL, PROJ_TN), lambda m, n: (0, n))]
    args = [x, norm_w.reshape(1, D_MODEL), w_bf]
    out_shape = [jax.ShapeDtypeStruct((n_tok, D_INNER), BF16),
                 jax.ShapeDtypeStruct((n_tok, D_INNER), F32),
                 jax.ShapeDtypeStruct((n_tok, D_INNER), F32),
                 jax.ShapeDtypeStruct((n_tok, D_INNER), BF16)]
    out_specs = [pl.BlockSpec((tm, PROJ_TN), out_map(j)) for j in range(4)]
    if with_f:
        in_specs += [pl.BlockSpec((D_MODEL, LANES), lambda m, n: (0, 0)),
                     pl.BlockSpec((1, LANES), lambda m, n: (0, 0))]
        args += [wf_bf, b_f]
        out_shape.append(jax.ShapeDtypeStruct((n_tok, LANES), F32))
        out_specs.append(pl.BlockSpec((tm, LANES), lambda m, n: (m, 0)))
    return pl.pallas_call(
        partial(_proj_kernel, with_f=with_f, n_per_out=n_per_out),
        grid=grid, in_specs=in_specs, out_specs=out_specs, out_shape=out_shape,
        scratch_shapes=[pltpu.VMEM((tm, D_MODEL), BF16)],
        compiler_params=_params(2),
        name="in_proj_fox" if with_f else "in_proj_sb",
    )(*args)


def _out_proj_kernel(*refs, final_norm):
    if final_norm:
        x_ref, y_ref, w_ref, nw_ref, o_ref = refs
    else:
        x_ref, y_ref, w_ref, o_ref = refs
    r = x_ref[...] + _dot(y_ref[...], w_ref[...])
    if final_norm:
        ms = jnp.mean(r * r, axis=-1, keepdims=True)
        r = (r * lax.rsqrt(ms + EPS)) * nw_ref[...]
    o_ref[...] = r


def _out_proj(x, y, w_bf, norm_w=None, *, tm):
    n_tok = x.shape[0]
    final_norm = norm_w is not None
    in_specs = [pl.BlockSpec((tm, D_MODEL), lambda m: (m, 0)),
                pl.BlockSpec((tm, D_INNER), lambda m: (m, 0)),
                pl.BlockSpec((D_INNER, D_MODEL), lambda m: (0, 0))]
    args = [x, y, w_bf]
    if final_norm:
        in_specs.append(pl.BlockSpec((1, D_MODEL), lambda m: (0, 0)))
        args.append(norm_w.reshape(1, D_MODEL))
    return pl.pallas_call(
        partial(_out_proj_kernel, final_norm=final_norm),
        grid=(n_tok // tm,), in_specs=in_specs,
        out_specs=pl.BlockSpec((tm, D_MODEL), lambda m: (m, 0)),
        out_shape=jax.ShapeDtypeStruct((n_tok, D_MODEL), F32),
        compiler_params=_params(1),
        name="out_proj_norm" if final_norm else "out_proj",
    )(*args)


def _cumsum_kernel(lf_ref, tri_ref, ccol_ref, crow_ref):
    n_chunks = lf_ref.shape[0] // CUM_T
    tri = tri_ref[...]
    carry = jnp.zeros((1, LANES), F32)
    for i in range(n_chunks):
        x1, x2, x3 = _split3(lf_ref[i * CUM_T:(i + 1) * CUM_T, :])
        c = (_dot(tri, x1) + _dot(tri, x2)) + _dot(tri, x3) + carry
        ccol_ref[i * CUM_T:(i + 1) * CUM_T, :] = c
        crow_ref[:, i * CUM_T:(i + 1) * CUM_T] = c.T[:N_HEADS, :]
        carry = c[CUM_T - 1:CUM_T, :]


def _cumsum(lf, tri_incl):
    b, t, _ = lf.shape
    return pl.pallas_call(
        _cumsum_kernel, grid=(b,),
        in_specs=[pl.BlockSpec((None, t, LANES), lambda i: (i, 0, 0)),
                  pl.BlockSpec((CUM_T, CUM_T), lambda i: (0, 0))],
        out_specs=[pl.BlockSpec((None, t, LANES), lambda i: (i, 0, 0)),
                   pl.BlockSpec((None, N_HEADS, t), lambda i: (i, 0, 0))],
        out_shape=[jax.ShapeDtypeStruct((b, t, LANES), F32),
                   jax.ShapeDtypeStruct((b, N_HEADS, t), F32)],
        compiler_params=_params(1), name="logf_cumsum",
    )(lf, tri_incl)


def _fox_kernel(q_ref, k_ref, v_ref, g_ref, ccol_ref, crow_ref, y_ref,
                kb_scr, vb_scr, m_scr, l_scr, acc_scr):
    h = pl.program_id(1)
    qi = pl.program_id(2)
    t = ATT_T

    @pl.when(qi == 0)
    def _():
        kb_scr[...] = k_ref[...].astype(BF16)
        vb_scr[...] = v_ref[...].astype(BF16)

    q = q_ref[...]
    lane = lax.broadcasted_iota(jnp.int32, (t, LANES), 1)
    cq = jnp.sum(jnp.where(lane == h, ccol_ref[...], 0.0), axis=1, keepdims=True)

    m_scr[...] = jnp.full((t, 1), NEG, F32)
    l_scr[...] = jnp.zeros((t, 1), F32)
    acc_scr[...] = jnp.zeros((t, HEAD_DIM), F32)

    def step(j, masked):
        off = pl.multiple_of(j * t, t)
        s = _dot_nt(q, kb_scr[pl.ds(off, t), :]) + (cq - crow_ref[pl.ds(j, 1), :])
        if masked:
            row = lax.broadcasted_iota(jnp.int32, (t, t), 0)
            col = lax.broadcasted_iota(jnp.int32, (t, t), 1)
            s = jnp.where(col <= row, s, NEG)
        m_prev = m_scr[...]
        m_new = jnp.maximum(m_prev, jnp.max(s, axis=1, keepdims=True))
        alpha = jnp.exp(m_prev - m_new)
        p = jnp.exp(s - m_new)
        l_scr[...] = alpha * l_scr[...] + jnp.sum(p, axis=1, keepdims=True)
        acc_scr[...] = alpha * acc_scr[...] + _dot(p.astype(BF16), vb_scr[pl.ds(off, t), :])
        m_scr[...] = m_new

    def body(j, carry):
        step(j, False)
        return carry

    lax.fori_loop(0, qi, body, 0)
    step(qi, True)

    o = acc_scr[...] / l_scr[...]
    y_ref[...] = (o * g_ref[...].astype(F32)).astype(BF16)


def _fox_attention(q, k, v, g, ccol, crow, *, batch, seq):
    nq = seq // ATT_T
    tok_spec = pl.BlockSpec((ATT_T, HEAD_DIM), lambda b, h, i: (b * nq + i, h))
    seq_spec = pl.BlockSpec((seq, HEAD_DIM), lambda b, h, i: (b, h))
    return pl.pallas_call(
        _fox_kernel, grid=(batch, N_HEADS, nq),
        in_specs=[tok_spec, seq_spec, seq_spec, tok_spec,
                  pl.BlockSpec((None, ATT_T, LANES), lambda b, h, i: (b, i, 0)),
                  pl.BlockSpec((None, None, nq, ATT_T), lambda b, h, i: (b, h, 0, 0))],
        out_specs=tok_spec,
        out_shape=jax.ShapeDtypeStruct((batch * seq, D_INNER), BF16),
        scratch_shapes=[pltpu.VMEM((seq, HEAD_DIM), BF16), pltpu.VMEM((seq, HEAD_DIM), BF16),
                        pltpu.VMEM((ATT_T, 1), F32), pltpu.VMEM((ATT_T, 1), F32),
                        pltpu.VMEM((ATT_T, HEAD_DIM), F32)],
        compiler_params=_params(3), name="fox_attention",
    )(q, k, v, g, ccol, crow.reshape(batch, N_HEADS, nq, ATT_T))


def _sb_weights(z, tri, r_prev, masked):
    sp = jnp.log1p(jnp.exp(-jnp.abs(z)))
    ls = jnp.minimum(z, 0.0) - sp
    u = ls - z
    if masked is not None:
        u = jnp.where(masked, u, 0.0)
    u1, u2 = _split2(u)
    after = _dot(u1, tri) + _dot(u2, tri) + r_prev
    a = jnp.exp(ls + after)
    if masked is not None:
        a = jnp.where(masked, a, 0.0)
    return a, r_prev + jnp.sum(u, axis=1, keepdims=True)


def _sb_kernel(q_ref, k_ref, v_ref, g_ref, tri_ref, y_ref, kb_scr, vb_scr, r_scr, acc_scr):
    qi = pl.program_id(2)
    t = ATT_T

    @pl.when(qi == 0)
    def _():
        kb_scr[...] = k_ref[...].astype(BF16)
        vb_scr[...] = v_ref[...].astype(BF16)

    q = q_ref[...]
    tri = tri_ref[...]
    row = lax.broadcasted_iota(jnp.int32, (t, t), 0)
    col = lax.broadcasted_iota(jnp.int32, (t, t), 1)

    off = pl.multiple_of(qi * t, t)
    z = _dot_nt(q, kb_scr[pl.ds(off, t), :])
    a, r = _sb_weights(z, tri, jnp.zeros((t, 1), F32), col < row)
    r_scr[...] = r
    acc_scr[...] = _dot(a.astype(BF16), vb_scr[pl.ds(off, t), :])

    def body(jj, carry):
        j = qi - 1 - jj
        off = pl.multiple_of(j * t, t)
        z = _dot_nt(q, kb_scr[pl.ds(off, t), :])
        a, r = _sb_weights(z, tri, r_scr[...], None)
        r_scr[...] = r
        acc_scr[...] += _dot(a.astype(BF16), vb_scr[pl.ds(off, t), :])
        return carry

    lax.fori_loop(0, qi, body, 0)
    y_ref[...] = (acc_scr[...] * g_ref[...].astype(F32)).astype(BF16)


def _sb_attention(q, k, v, g, tri_strict, *, batch, seq):
    nq = seq // ATT_T
    tok_spec = pl.BlockSpec((ATT_T, HEAD_DIM), lambda b, h, i: (b * nq + i, h))
    seq_spec = pl.BlockSpec((seq, HEAD_DIM), lambda b, h, i: (b, h))
    return pl.pallas_call(
        _sb_kernel, grid=(batch, N_HEADS, nq),
        in_specs=[tok_spec, seq_spec, seq_spec, tok_spec,
                  pl.BlockSpec((ATT_T, ATT_T), lambda b, h, i: (0, 0))],
        out_specs=tok_spec,
        out_shape=jax.ShapeDtypeStruct((batch * seq, D_INNER), BF16),
        scratch_shapes=[pltpu.VMEM((seq, HEAD_DIM), BF16), pltpu.VMEM((seq, HEAD_DIM), BF16),
                        pltpu.VMEM((ATT_T, 1), F32), pltpu.VMEM((ATT_T, HEAD_DIM), F32)],
        compiler_params=_params(3), name="sb_attention",
    )(q, k, v, g, tri_strict)


def _gather_keys(dst, past_ref, new_ref, hh, past_len, n_new):
    sl = slice(hh * HEAD_DIM, (hh + 1) * HEAD_DIM)
    dst[0:past_len, :] = past_ref[:, sl].astype(BF16)
    pad = jnp.zeros((SAMPLE_KPAD - n_new, HEAD_DIM), F32)
    dst[past_len:past_len + SAMPLE_KPAD, :] = jnp.concatenate(
        [new_ref[:, sl], pad], axis=0).astype(BF16)


def _fox_sample_kernel(q_ref, kn_ref, vn_ref, g_ref, kp_ref, vp_ref, lft_ref, tri_ref, y_ref,
                       kb_scr, vb_scr, c_scr, *, past_len, n_new):
    hg = pl.program_id(1)
    kpad = past_len + SAMPLE_KPAD
    tri = tri_ref[...]
    x1, x2, x3 = _split3(lft_ref[...])
    c_scr[...] = (_dot(x1, tri) + _dot(x2, tri)) + _dot(x3, tri)

    row = lax.broadcasted_iota(jnp.int32, (n_new, kpad), 0)
    col = lax.broadcasted_iota(jnp.int32, (n_new, kpad), 1)
    qpos = row + past_len
    for hh in range(HEAD_GROUP):
        sl = slice(hh * HEAD_DIM, (hh + 1) * HEAD_DIM)
        _gather_keys(kb_scr, kp_ref, kn_ref, hh, past_len, n_new)
        _gather_keys(vb_scr, vp_ref, vn_ref, hh, past_len, n_new)
        ck = c_scr[pl.ds(hg * HEAD_GROUP + hh, 1), :]
        cq = jnp.sum(jnp.where(col == qpos, ck, 0.0), axis=1, keepdims=True)
        s = _dot_nt(q_ref[:, sl], kb_scr[...]) + (cq - ck)
        s = jnp.where(col <= qpos, s, NEG)
        m = jnp.max(s, axis=1, keepdims=True)
        p = jnp.exp(s - m)
        l = jnp.sum(p, axis=1, keepdims=True)
        o = _dot(p.astype(BF16), vb_scr[...]) / l
        y_ref[:, sl] = (o * g_ref[:, sl].astype(F32)).astype(BF16)


def _sb_sample_kernel(q_ref, kn_ref, vn_ref, g_ref, kp_ref, vp_ref, tri_ref, y_ref,
                      kb_scr, vb_scr, *, past_len, n_new):
    kpad = past_len + SAMPLE_KPAD
    tri = tri_ref[...]
    row = lax.broadcasted_iota(jnp.int32, (n_new, kpad), 0)
    col = lax.broadcasted_iota(jnp.int32, (n_new, kpad), 1)
    mask = col < row + past_len
    for hh in range(HEAD_GROUP):
        sl = slice(hh * HEAD_DIM, (hh + 1) * HEAD_DIM)
        _gather_keys(kb_scr, kp_ref, kn_ref, hh, past_len, n_new)
        _gather_keys(vb_scr, vp_ref, vn_ref, hh, past_len, n_new)
        z = _dot_nt(q_ref[:, sl], kb_scr[...])
        a, _ = _sb_weights(z, tri, jnp.zeros((n_new, 1), F32), mask)
        o = _dot(a.astype(BF16), vb_scr[...])
        y_ref[:, sl] = (o * g_ref[:, sl].astype(F32)).astype(BF16)


def _sample_attention(kind, q, k_new, v_new, g, k_past, v_past, tri, lf_all_t=None):
    batch, past_len = k_past.shape[0], k_past.shape[1]
    n_new = q.shape[0] // batch
    kpad = past_len + SAMPLE_KPAD
    gw = HEAD_GROUP * HEAD_DIM
    tok_spec = pl.BlockSpec((n_new, gw), lambda b, hg: (b, hg))
    past_spec = pl.BlockSpec((None, past_len, gw), lambda b, hg: (b, 0, hg))
    tri_spec = pl.BlockSpec((kpad, kpad), lambda b, hg: (0, 0))
    kp = k_past.reshape(batch, past_len, D_INNER)
    vp = v_past.reshape(batch, past_len, D_INNER)
    scratch = [pltpu.VMEM((kpad, HEAD_DIM), BF16), pltpu.VMEM((kpad, HEAD_DIM), BF16)]
    if kind == "fox":
        body = partial(_fox_sample_kernel, past_len=past_len, n_new=n_new)
        in_specs = [tok_spec, tok_spec, tok_spec, tok_spec, past_spec, past_spec,
                    pl.BlockSpec((None, N_HEADS, kpad), lambda b, hg: (b, 0, 0)), tri_spec]
        args = (q, k_new, v_new, g, kp, vp, lf_all_t, tri)
        scratch.append(pltpu.VMEM((N_HEADS, kpad), F32))
    else:
        body = partial(_sb_sample_kernel, past_len=past_len, n_new=n_new)
        in_specs = [tok_spec, tok_spec, tok_spec, tok_spec, past_spec, past_spec, tri_spec]
        args = (q, k_new, v_new, g, kp, vp, tri)
    return pl.pallas_call(
        body, grid=(batch, N_HEADS // HEAD_GROUP), in_specs=in_specs, out_specs=tok_spec,
        out_shape=jax.ShapeDtypeStruct((batch * n_new, D_INNER), BF16),
        scratch_shapes=scratch, compiler_params=_params(2), name=kind + "_sample_attention",
    )(*args)


def _tri(n, strict, lower):
    r = lax.broadcasted_iota(jnp.int32, (n, n), 0)
    c = lax.broadcasted_iota(jnp.int32, (n, n), 1)
    if lower:
        keep = (r > c) if strict else (r >= c)
    else:
        keep = (r < c) if strict else (r <= c)
    return keep.astype(BF16)


def kernel(x_prompt, x_sample, cache_fox_k, cache_fox_v, cache_fox_logf, cache_sb_k, cache_sb_v,
           norm_0, w_in_0, b_f_0, w_out_0, norm_1, w_in_1, w_out_1, norm_f):
    batch, seq, _ = x_prompt.shape
    dec_batch, dec_seq, _ = x_sample.shape
    past_len = cache_fox_k.shape[1]
    n_p, n_s = batch * seq, dec_batch * dec_seq
    kpad = past_len + SAMPLE_KPAD

    w0 = w_in_0[:, :4 * D_INNER].astype(BF16)
    wf = jnp.pad(w_in_0[:, 4 * D_INNER:], ((0, 0), (0, LANES - N_HEADS))).astype(BF16)
    bf = jnp.pad(b_f_0, (0, LANES - N_HEADS)).reshape(1, LANES)
    w1 = w_in_1.astype(BF16)
    wo0 = w_out_0.astype(BF16)
    wo1 = w_out_1.astype(BF16)

    xp = x_prompt.reshape(n_p, D_MODEL)
    xs = x_sample.reshape(n_s, D_MODEL)

    qp, kp0, vp0, gp, lfp = _in_proj(xp, norm_0, w0, wf, bf, tm=512)
    qs, ks0, vs0, gs, lfs = _in_proj(xs, norm_0, w0, wf, bf, tm=n_s)
    lfp = lfp.reshape(batch, seq, LANES)
    ccol, crow = _cumsum(lfp, _tri(CUM_T, strict=False, lower=True))
    yp = _fox_attention(qp, kp0, vp0, gp, ccol, crow, batch=batch, seq=seq)
    lfs = lfs[:, :N_HEADS].reshape(dec_batch, dec_seq, N_HEADS)
    lf_all = jnp.concatenate([cache_fox_logf, lfs], axis=1)
    lf_all_t = jnp.pad(jnp.swapaxes(lf_all, 1, 2), ((0, 0), (0, 0), (0, kpad - past_len - dec_seq)))
    ys = _sample_attention("fox", qs, ks0, vs0, gs, cache_fox_k, cache_fox_v,
                           _tri(kpad, strict=False, lower=False), lf_all_t)
    xp1 = _out_proj(xp, yp, wo0, tm=512)
    xs1 = _out_proj(xs, ys, wo0, tm=n_s)

    qp, kp1, vp1, gp = _in_proj(xp1, norm_1, w1, tm=512)
    qs, ks1, vs1, gs = _in_proj(xs1, norm_1, w1, tm=n_s)
    yp = _sb_attention(qp, kp1, vp1, gp, _tri(ATT_T, strict=True, lower=True), batch=batch, seq=seq)
    ys = _sample_attention("sb", qs, ks1, vs1, gs, cache_sb_k, cache_sb_v,
                           _tri(kpad, strict=True, lower=True))
    y_prompt = _out_proj(xp1, yp, wo1, norm_f, tm=512)
    y_sample = _out_proj(xs1, ys, wo1, norm_f, tm=n_s)

    hp = (batch, seq, N_HEADS, HEAD_DIM)
    hs = (dec_batch, dec_seq, N_HEADS, HEAD_DIM)
    return (y_prompt.reshape(batch, seq, D_MODEL), y_sample.reshape(dec_batch, dec_seq, D_MODEL),
            kp0.reshape(hp), vp0.reshape(hp), lfp[:, :, :N_HEADS],
            kp1.reshape(hp), vp1.reshape(hp),
            ks0.reshape(hs), vs0.reshape(hs), lfs,
            ks1.reshape(hs), vs1.reshape(hs))
```

```python
from functools import partial

import jax
import jax.numpy as jnp
from jax import lax
from jax.experimental import pallas as pl
from jax.experimental.pallas import tpu as pltpu

D_MODEL = 2048
N_HEADS = 16
HEAD_DIM = 128
D_INNER = N_HEADS * HEAD_DIM
EPS = 1e-6
SCALE = HEAD_DIM ** -0.5
LOG2E = 1.4426950408889634
NEG = -1e30

LANES = 128
F32 = jnp.float32
BF16 = jnp.bfloat16

VMEM_LIMIT = 56 * 1024 * 1024

PROJ_TN = 512
ATT_T = 512
ATT_RC = 32
SB_SEG = 256
CUM_T = 512
HEAD_GROUP = 4
SAMPLE_KPAD = 128


def _params(n_axes):
    return pltpu.CompilerParams(dimension_semantics=("arbitrary",) * n_axes,
                                vmem_limit_bytes=VMEM_LIMIT)


def _log_sigmoid(z):
    return jnp.minimum(z, 0.0) - jnp.log1p(jnp.exp(-jnp.abs(z)))


def _split3(x):
    x1 = x.astype(BF16)
    r1 = x - x1.astype(F32)
    x2 = r1.astype(BF16)
    x3 = (r1 - x2.astype(F32)).astype(BF16)
    return x1, x2, x3


def _split2(x):
    x1 = x.astype(BF16)
    x2 = (x - x1.astype(F32)).astype(BF16)
    return x1, x2


def _dot(a, b):
    return jnp.dot(a, b, preferred_element_type=F32)


def _dot_nt(a, b):
    return lax.dot_general(a, b, (((1,), (1,)), ((), ())), preferred_element_type=F32)


def _proj_kernel(*refs, with_f, n_per_out):
    if with_f:
        (x_ref, nw_ref, w_ref, wf_ref, bf_ref,
         q_ref, k_ref, v_ref, g_ref, lf_ref, xn_scr) = refs
    else:
        x_ref, nw_ref, w_ref, q_ref, k_ref, v_ref, g_ref, xn_scr = refs
    n = pl.program_id(1)

    @pl.when(n == 0)
    def _():
        x = x_ref[...]
        ms = jnp.mean(x * x, axis=-1, keepdims=True)
        xn = (x * lax.rsqrt(ms + EPS)) * nw_ref[...]
        xn_scr[...] = xn.astype(BF16)
        if with_f:
            z = _dot(xn_scr[...], wf_ref[...]) + bf_ref[...]
            lf_ref[...] = _log_sigmoid(z)

    acc = _dot(xn_scr[...], w_ref[...])
    which = n // n_per_out

    @pl.when(which == 0)
    def _():
        q_ref[...] = (acc * (SCALE * LOG2E)).astype(BF16)

    @pl.when(which == 1)
    def _():
        k_ref[...] = acc

    @pl.when(which == 2)
    def _():
        v_ref[...] = acc

    @pl.when(which == 3)
    def _():
        g_ref[...] = (acc * jax.nn.sigmoid(acc)).astype(BF16)


def _in_proj(x, norm_w, w_bf, wf_bf=None, b_f=None, *, tm):
    n_tok = x.shape[0]
    with_f = wf_bf is not None
    n_per_out = D_INNER // PROJ_TN
    grid = (n_tok // tm, 4 * n_per_out)

    def out_map(j):
        return lambda m, n: (m, jnp.clip(n - j * n_per_out, 0, n_per_out - 1))

    in_specs = [pl.BlockSpec((tm, D_MODEL), lambda m, n: (m, 0)),
                pl.BlockSpec((1, D_MODEL), lambda m, n: (0, 0)),
                pl.BlockSpec((D_MODEL, PROJ_TN), lambda m, n: (0, n))]
    args = [x, norm_w.reshape(1, D_MODEL), w_bf]
    out_shape = [jax.ShapeDtypeStruct((n_tok, D_INNER), BF16),
                 jax.ShapeDtypeStruct((n_tok, D_INNER), F32),
                 jax.ShapeDtypeStruct((n_tok, D_INNER), F32),
                 jax.ShapeDtypeStruct((n_tok, D_INNER), BF16)]
    out_specs = [pl.BlockSpec((tm, PROJ_TN), out_map(j)) for j in range(4)]
    if with_f:
        in_specs += [pl.BlockSpec((D_MODEL, LANES), lambda m, n: (0, 0)),
                     pl.BlockSpec((1, LANES), lambda m, n: (0, 0))]
        args += [wf_bf, b_f]
        out_shape.append(jax.ShapeDtypeStruct((n_tok, LANES), F32))
        out_specs.append(pl.BlockSpec((tm, LANES), lambda m, n: (m, 0)))
    return pl.pallas_call(
        partial(_proj_kernel, with_f=with_f, n_per_out=n_per_out),
        grid=grid, in_specs=in_specs, out_specs=out_specs, out_shape=out_shape,
        scratch_shapes=[pltpu.VMEM((tm, D_MODEL), BF16)],
        compiler_params=_params(2),
        name="in_proj_fox" if with_f else "in_proj_sb",
    )(*args)


def _out_proj_kernel(*refs, final_norm):
    if final_norm:
        x_ref, y_ref, w_ref, nw_ref, o_ref = refs
    else:
        x_ref, y_ref, w_ref, o_ref = refs
    r = x_ref[...] + _dot(y_ref[...], w_ref[...])
    if final_norm:
        ms = jnp.mean(r * r, axis=-1, keepdims=True)
        r = (r * lax.rsqrt(ms + EPS)) * nw_ref[...]
    o_ref[...] = r


def _out_proj(x, y, w_bf, norm_w=None, *, tm):
    n_tok = x.shape[0]
    final_norm = norm_w is not None
    in_specs = [pl.BlockSpec((tm, D_MODEL), lambda m: (m, 0)),
                pl.BlockSpec((tm, D_INNER), lambda m: (m, 0)),
                pl.BlockSpec((D_INNER, D_MODEL), lambda m: (0, 0))]
    args = [x, y, w_bf]
    if final_norm:
        in_specs.append(pl.BlockSpec((1, D_MODEL), lambda m: (0, 0)))
        args.append(norm_w.reshape(1, D_MODEL))
    return pl.pallas_call(
        partial(_out_proj_kernel, final_norm=final_norm),
        grid=(n_tok // tm,), in_specs=in_specs,
        out_specs=pl.BlockSpec((tm, D_MODEL), lambda m: (m, 0)),
        out_shape=jax.ShapeDtypeStruct((n_tok, D_MODEL), F32),
        compiler_params=_params(1),
        name="out_proj_norm" if final_norm else "out_proj",
    )(*args)


def _cumsum_kernel(lf_ref, tri_ref, c_ref):
    n_chunks = lf_ref.shape[0] // CUM_T
    tri = tri_ref[...]
    carry = jnp.zeros((1, LANES), F32)
    for i in range(n_chunks):
        x1, x2, x3 = _split3(lf_ref[i * CUM_T:(i + 1) * CUM_T, :])
        c = (_dot(tri, x1) + _dot(tri, x2)) + _dot(tri, x3) + carry
        c_ref[i * CUM_T:(i + 1) * CUM_T, :] = c
        carry = c[CUM_T - 1:CUM_T, :]


def _cumsum(lf, tri_incl):
    b, t, _ = lf.shape
    return pl.pallas_call(
        _cumsum_kernel, grid=(b,),
        in_specs=[pl.BlockSpec((None, t, LANES), lambda i: (i, 0, 0)),
                  pl.BlockSpec((CUM_T, CUM_T), lambda i: (0, 0))],
        out_specs=pl.BlockSpec((None, t, LANES), lambda i: (i, 0, 0)),
        out_shape=jax.ShapeDtypeStruct((b, t, LANES), F32),
        compiler_params=_params(1), name="logf_cumsum",
    )(lf, tri_incl)


def _decay_columns(c_tile, h, for_keys):
    n = c_tile.shape[0]
    lane = lax.broadcasted_iota(jnp.int32, (n, LANES), 1)
    c = jnp.sum(jnp.where(lane == h, c_tile, 0.0), axis=1, keepdims=True) * LOG2E
    if for_keys:
        c = -c
    first_c, first_one = (3, 0) if for_keys else (0, 3)
    out = jnp.where((lane >= first_one) & (lane < first_one + 3), 1.0, 0.0)
    for i, part in enumerate(_split3(c)):
        out = jnp.where(lane == first_c + i, part.astype(F32), out)
    return out.astype(BF16)


def _tree(op, xs):
    while len(xs) > 1:
        xs = [op(xs[i], xs[i + 1]) for i in range(0, len(xs) - 1, 2)] + ([xs[-1]] if len(xs) % 2 else [])
    return xs[0]


def _fox_kernel(q_ref, k_ref, v_ref, g_ref, cq_ref, ck_ref, y_ref,
                kb_scr, vb_scr, qa_scr, s_scr, p_scr, m_scr, l_scr, acc_scr):
    h = pl.program_id(1)
    qi = pl.program_id(2)
    t = ATT_T
    seq = k_ref.shape[0]

    @pl.when(qi == 0)
    def _():
        def fill(i, carry):
            rows = pl.ds(pl.multiple_of(i * t, t), t)
            kb_scr[rows, 0:HEAD_DIM] = k_ref[rows, :].astype(BF16)
            kb_scr[rows, HEAD_DIM:] = _decay_columns(ck_ref[rows, :], h, True)
            vb_scr[rows, :] = v_ref[rows, :].astype(BF16)
            return carry
        lax.fori_loop(0, seq // t, fill, 0)

    qa_scr[:, 0:HEAD_DIM] = q_ref[...]
    qa_scr[:, HEAD_DIM:] = _decay_columns(cq_ref[...], h, False)
    m_scr[...] = jnp.full((t, LANES), NEG, F32)
    l_scr[...] = jnp.zeros((t, LANES), F32)
    acc_scr[...] = jnp.zeros((t, HEAD_DIM), F32)

    def scores(j, slot):
        off = pl.multiple_of(j * t, t)
        s_scr[slot] = _dot_nt(qa_scr[...], kb_scr[pl.ds(off, t), :])

    def softmax(slot, masked):
        for r in range(0, t, ATT_RC):
            rows = slice(r, r + ATT_RC)
            s = s_scr[slot, rows, :]
            if masked:
                row = lax.broadcasted_iota(jnp.int32, (ATT_RC, t), 0) + r
                col = lax.broadcasted_iota(jnp.int32, (ATT_RC, t), 1)
                s = jnp.where(col <= row, s, NEG)
            slabs = [s[:, c:c + LANES] for c in range(0, t, LANES)]
            m_prev = m_scr[rows, :]
            m_new = jnp.maximum(m_prev, jnp.max(_tree(jnp.maximum, slabs), axis=1, keepdims=True))
            alpha = jnp.exp2(m_prev - m_new)
            ps = [jnp.exp2(sl - m_new) for sl in slabs]
            l_scr[rows, :] = alpha * l_scr[rows, :] + _tree(jnp.add, ps)
            m_scr[rows, :] = m_new
            acc_scr[rows, :] = alpha * acc_scr[rows, :]
            p_scr[rows, :] = jnp.concatenate(ps, axis=1).astype(BF16)

    def weighted_values(j):
        off = pl.multiple_of(j * t, t)
        acc_scr[...] += _dot(p_scr[...], vb_scr[pl.ds(off, t), :])

    scores(0, 0)

    def body(jj, carry):
        a = 2 * jj
        scores(a + 1, 1)
        softmax(0, False)
        weighted_values(a)
        scores(a + 2, 0)
        softmax(1, False)
        weighted_values(a + 1)
        return carry

    lax.fori_loop(0, qi // 2, body, 0)

    @pl.when(qi % 2 == 0)
    def _():
        softmax(0, True)
        weighted_values(qi)

    @pl.when(qi % 2 == 1)
    def _():
        scores(qi, 1)
        softmax(0, False)
        weighted_values(qi - 1)
        softmax(1, True)
        weighted_values(qi)

    o = acc_scr[...] / jnp.sum(l_scr[...], axis=1, keepdims=True)
    y_ref[...] = (o * g_ref[...].astype(F32)).astype(BF16)


def _fox_attention(q, k, v, g, c, *, batch, seq):
    nq = seq // ATT_T
    tok_spec = pl.BlockSpec((ATT_T, HEAD_DIM), lambda b, h, i: (b * nq + i, h))
    seq_spec = pl.BlockSpec((seq, HEAD_DIM), lambda b, h, i: (b, h))
    return pl.pallas_call(
        _fox_kernel, grid=(batch, N_HEADS, nq),
        in_specs=[tok_spec, seq_spec, seq_spec, tok_spec,
                  pl.BlockSpec((None, ATT_T, LANES), lambda b, h, i: (b, i, 0)),
                  pl.BlockSpec((None, seq, LANES), lambda b, h, i: (b, 0, 0))],
        out_specs=tok_spec,
        out_shape=jax.ShapeDtypeStruct((batch * seq, D_INNER), BF16),
        scratch_shapes=[pltpu.VMEM((seq, 2 * HEAD_DIM), BF16), pltpu.VMEM((seq, HEAD_DIM), BF16),
                        pltpu.VMEM((ATT_T, 2 * HEAD_DIM), BF16),
                        pltpu.VMEM((2, ATT_T, ATT_T), F32), pltpu.VMEM((ATT_T, ATT_T), BF16),
                        pltpu.VMEM((ATT_T, LANES), F32), pltpu.VMEM((ATT_T, LANES), F32),
                        pltpu.VMEM((ATT_T, HEAD_DIM), F32)],
        compiler_params=_params(3), name="fox_attention",
    )(q, k, v, g, c, c)


def _sb_weights(z, tri, r_prev, masked):
    sp = jnp.log2(1.0 + jnp.exp2(-jnp.abs(z)))
    ls = jnp.minimum(z, 0.0) - sp
    u = ls - z
    if masked is not None:
        u = jnp.where(masked, u, 0.0)
    u1, u2 = _split2(u)
    after = _dot(u1, tri) + _dot(u2, tri) + r_prev
    a = jnp.exp2(ls + after)
    if masked is not None:
        a = jnp.where(masked, a, 0.0)
    return a, r_prev + jnp.sum(u, axis=1, keepdims=True)


def _sb_kernel(q_ref, k_ref, v_ref, g_ref, tri_ref, y_ref,
               kb_scr, vb_scr, z_scr, cs_scr, u_scr, a_scr, rs_scr, r_scr, acc_scr):
    qi = pl.program_id(2)
    t = ATT_T
    seg = SB_SEG
    n_seg = t // seg

    @pl.when(qi == 0)
    def _():
        kb_scr[...] = k_ref[...].astype(BF16)
        vb_scr[...] = v_ref[...].astype(BF16)

    r_scr[...] = jnp.zeros((t, LANES), F32)
    acc_scr[...] = jnp.zeros((t, HEAD_DIM), F32)

    def scores(j, k):
        off = pl.multiple_of(j * t, t)
        z_scr[k] = _dot_nt(q_ref[...], kb_scr[pl.ds(off, t), :])

    def strict_mask(r):
        row = lax.broadcasted_iota(jnp.int32, (ATT_RC, t), 0) + r
        col = lax.broadcasted_iota(jnp.int32, (ATT_RC, t), 1)
        return col < row

    def log_gates(k, masked):
        for r in range(0, t, ATT_RC):
            rows = slice(r, r + ATT_RC)
            z = z_scr[k, rows, :]
            lg = jnp.log2(1.0 + jnp.exp2(-jnp.abs(z)))
            lsig = jnp.minimum(z, 0.0) - lg
            u = lsig - z
            if masked:
                u = jnp.where(strict_mask(r), u, 0.0)
            z_scr[k, rows, :] = lsig
            u_scr[k, rows, :] = u.astype(BF16)
            for sgi in range(n_seg):
                slabs = [u[:, c:c + LANES] for c in range(sgi * seg, (sgi + 1) * seg, LANES)]
                rs_scr[k, sgi, rows, :] = jnp.broadcast_to(
                    jnp.sum(_tree(jnp.add, slabs), axis=1, keepdims=True), (ATT_RC, LANES))

    def suffix_sums(k):
        for sgi in range(n_seg):
            cols = slice(sgi * seg, (sgi + 1) * seg)
            cs_scr[k, :, cols] = _dot(u_scr[k, :, cols], tri_ref[...])

    def weights(k, masked):
        for r in range(0, t, ATT_RC):
            rows = slice(r, r + ATT_RC)
            base = r_scr[rows, :]
            x = z_scr[k, rows, :] + cs_scr[k, rows, :]
            parts = []
            for sgi in reversed(range(n_seg)):
                for c in reversed(range(sgi * seg, (sgi + 1) * seg, LANES)):
                    parts.append(jnp.exp2(x[:, c:c + LANES] + base))
                base = base + rs_scr[k, sgi, rows, :]
            a = jnp.concatenate(parts[::-1], axis=1)
            if masked:
                a = jnp.where(strict_mask(r), a, 0.0)
            a_scr[k, rows, :] = a.astype(BF16)
            r_scr[rows, :] = base

    def weighted_values(j, k):
        off = pl.multiple_of(j * t, t)
        acc_scr[...] += _dot(a_scr[k], vb_scr[pl.ds(off, t), :])

    scores(qi, 0)
    log_gates(0, True)
    suffix_sums(0)
    weights(0, True)
    weighted_values(qi, 0)
    scores(jnp.maximum(qi - 1, 0), 0)

    def body(jj, carry):
        ja = qi - 1 - 2 * jj
        scores(ja - 1, 1)
        log_gates(0, False)
        suffix_sums(0)
        log_gates(1, False)
        suffix_sums(1)
        weights(0, False)
        weighted_values(ja, 0)
        scores(jnp.maximum(ja - 2, 0), 0)
        weights(1, False)
        weighted_values(ja - 1, 1)
        return carry

    lax.fori_loop(0, qi // 2, body, 0)

    @pl.when(qi % 2 == 1)
    def _():
        log_gates(0, False)
        suffix_sums(0)
        weights(0, False)
        weighted_values(0, 0)

    y_ref[...] = (acc_scr[...] * g_ref[...].astype(F32)).astype(BF16)


def _sb_attention(q, k, v, g, tri_strict, *, batch, seq):
    nq = seq // ATT_T
    tok_spec = pl.BlockSpec((ATT_T, HEAD_DIM), lambda b, h, i: (b * nq + i, h))
    seq_spec = pl.BlockSpec((seq, HEAD_DIM), lambda b, h, i: (b, h))
    return pl.pallas_call(
        _sb_kernel, grid=(batch, N_HEADS, nq),
        in_specs=[tok_spec, seq_spec, seq_spec, tok_spec,
                  pl.BlockSpec((SB_SEG, SB_SEG), lambda b, h, i: (0, 0))],
        out_specs=tok_spec,
        out_shape=jax.ShapeDtypeStruct((batch * seq, D_INNER), BF16),
        scratch_shapes=[pltpu.VMEM((seq, HEAD_DIM), BF16), pltpu.VMEM((seq, HEAD_DIM), BF16),
                        pltpu.VMEM((2, ATT_T, ATT_T), F32), pltpu.VMEM((2, ATT_T, ATT_T), F32),
                        pltpu.VMEM((2, ATT_T, ATT_T), BF16), pltpu.VMEM((2, ATT_T, ATT_T), BF16),
                        pltpu.VMEM((2, ATT_T // SB_SEG, ATT_T, LANES), F32),
                        pltpu.VMEM((ATT_T, LANES), F32), pltpu.VMEM((ATT_T, HEAD_DIM), F32)],
        compiler_params=_params(3), name="sb_attention",
    )(q, k, v, g, tri_strict)


def _gather_keys(dst, past_ref, new_ref, hh, past_len, n_new):
    sl = slice(hh * HEAD_DIM, (hh + 1) * HEAD_DIM)
    dst[0:past_len, :] = past_ref[:, sl].astype(BF16)
    pad = jnp.zeros((SAMPLE_KPAD - n_new, HEAD_DIM), F32)
    dst[past_len:past_len + SAMPLE_KPAD, :] = jnp.concatenate(
        [new_ref[:, sl], pad], axis=0).astype(BF16)


def _fox_sample_kernel(q_ref, kn_ref, vn_ref, g_ref, kp_ref, vp_ref, lft_ref, tri_ref, y_ref,
                       kb_scr, vb_scr, c_scr, *, past_len, n_new):
    hg = pl.program_id(1)
    kpad = past_len + SAMPLE_KPAD
    tri = tri_ref[...]
    x1, x2, x3 = _split3(lft_ref[...])
    c_scr[...] = (_dot(x1, tri) + _dot(x2, tri)) + _dot(x3, tri)

    row = lax.broadcasted_iota(jnp.int32, (n_new, kpad), 0)
    col = lax.broadcasted_iota(jnp.int32, (n_new, kpad), 1)
    qpos = row + past_len
    for hh in range(HEAD_GROUP):
        sl = slice(hh * HEAD_DIM, (hh + 1) * HEAD_DIM)
        _gather_keys(kb_scr, kp_ref, kn_ref, hh, past_len, n_new)
        _gather_keys(vb_scr, vp_ref, vn_ref, hh, past_len, n_new)
        ck = c_scr[pl.ds(hg * HEAD_GROUP + hh, 1), :] * LOG2E
        cq = jnp.sum(jnp.where(col == qpos, ck, 0.0), axis=1, keepdims=True)
        s = _dot_nt(q_ref[:, sl], kb_scr[...]) + (cq - ck)
        s = jnp.where(col <= qpos, s, NEG)
        m = jnp.max(s, axis=1, keepdims=True)
        p = jnp.exp2(s - m)
        l = jnp.sum(p, axis=1, keepdims=True)
        o = _dot(p.astype(BF16), vb_scr[...]) / l
        y_ref[:, sl] = (o * g_ref[:, sl].astype(F32)).astype(BF16)


def _sb_sample_kernel(q_ref, kn_ref, vn_ref, g_ref, kp_ref, vp_ref, tri_ref, y_ref,
                      kb_scr, vb_scr, *, past_len, n_new):
    kpad = past_len + SAMPLE_KPAD
    tri = tri_ref[...]
    row = lax.broadcasted_iota(jnp.int32, (n_new, kpad), 0)
    col = lax.broadcasted_iota(jnp.int32, (n_new, kpad), 1)
    mask = col < row + past_len
    for hh in range(HEAD_GROUP):
        sl = slice(hh * HEAD_DIM, (hh + 1) * HEAD_DIM)
        _gather_keys(kb_scr, kp_ref, kn_ref, hh, past_len, n_new)
        _gather_keys(vb_scr, vp_ref, vn_ref, hh, past_len, n_new)
        z = _dot_nt(q_ref[:, sl], kb_scr[...])
        a, _ = _sb_weights(z, tri, jnp.zeros((n_new, 1), F32), mask)
        o = _dot(a.astype(BF16), vb_scr[...])
        y_ref[:, sl] = (o * g_ref[:, sl].astype(F32)).astype(BF16)


def _sample_attention(kind, q, k_new, v_new, g, k_past, v_past, tri, lf_all_t=None):
    batch, past_len = k_past.shape[0], k_past.shape[1]
    n_new = q.shape[0] // batch
    kpad = past_len + SAMPLE_KPAD
    gw = HEAD_GROUP * HEAD_DIM
    tok_spec = pl.BlockSpec((n_new, gw), lambda b, hg: (b, hg))
    past_spec = pl.BlockSpec((None, past_len, gw), lambda b, hg: (b, 0, hg))
    tri_spec = pl.BlockSpec((kpad, kpad), lambda b, hg: (0, 0))
    kp = k_past.reshape(batch, past_len, D_INNER)
    vp = v_past.reshape(batch, past_len, D_INNER)
    scratch = [pltpu.VMEM((kpad, HEAD_DIM), BF16), pltpu.VMEM((kpad, HEAD_DIM), BF16)]
    if kind == "fox":
        body = partial(_fox_sample_kernel, past_len=past_len, n_new=n_new)
        in_specs = [tok_spec, tok_spec, tok_spec, tok_spec, past_spec, past_spec,
                    pl.BlockSpec((None, N_HEADS, kpad), lambda b, hg: (b, 0, 0)), tri_spec]
        args = (q, k_new, v_new, g, kp, vp, lf_all_t, tri)
        scratch.append(pltpu.VMEM((N_HEADS, kpad), F32))
    else:
        body = partial(_sb_sample_kernel, past_len=past_len, n_new=n_new)
        in_specs = [tok_spec, tok_spec, tok_spec, tok_spec, past_spec, past_spec, tri_spec]
        args = (q, k_new, v_new, g, kp, vp, tri)
    return pl.pallas_call(
        body, grid=(batch, N_HEADS // HEAD_GROUP), in_specs=in_specs, out_specs=tok_spec,
        out_shape=jax.ShapeDtypeStruct((batch * n_new, D_INNER), BF16),
        scratch_shapes=scratch, compiler_params=_params(2), name=kind + "_sample_attention",
    )(*args)


def _tri(n, strict, lower):
    r = lax.broadcasted_iota(jnp.int32, (n, n), 0)
    c = lax.broadcasted_iota(jnp.int32, (n, n), 1)
    if lower:
        keep = (r > c) if strict else (r >= c)
    else:
        keep = (r < c) if strict else (r <= c)
    return keep.astype(BF16)


def kernel(x_prompt, x_sample, cache_fox_k, cache_fox_v, cache_fox_logf, cache_sb_k, cache_sb_v,
           norm_0, w_in_0, b_f_0, w_out_0, norm_1, w_in_1, w_out_1, norm_f):
    batch, seq, _ = x_prompt.shape
    dec_batch, dec_seq, _ = x_sample.shape
    past_len = cache_fox_k.shape[1]
    n_p, n_s = batch * seq, dec_batch * dec_seq
    kpad = past_len + SAMPLE_KPAD

    w0 = w_in_0[:, :4 * D_INNER].astype(BF16)
    wf = jnp.pad(w_in_0[:, 4 * D_INNER:], ((0, 0), (0, LANES - N_HEADS))).astype(BF16)
    bf = jnp.pad(b_f_0, (0, LANES - N_HEADS)).reshape(1, LANES)
    w1 = w_in_1.astype(BF16)
    wo0 = w_out_0.astype(BF16)
    wo1 = w_out_1.astype(BF16)

    xp = x_prompt.reshape(n_p, D_MODEL)
    xs = x_sample.reshape(n_s, D_MODEL)

    qp, kp0, vp0, gp, lfp = _in_proj(xp, norm_0, w0, wf, bf, tm=512)
    qs, ks0, vs0, gs, lfs = _in_proj(xs, norm_0, w0, wf, bf, tm=n_s)
    lfp = lfp.reshape(batch, seq, LANES)
    cum = _cumsum(lfp, _tri(CUM_T, strict=False, lower=True))
    yp = _fox_attention(qp, kp0, vp0, gp, cum, batch=batch, seq=seq)
    lfs = lfs[:, :N_HEADS].reshape(dec_batch, dec_seq, N_HEADS)
    lf_all = jnp.concatenate([cache_fox_logf, lfs], axis=1)
    lf_all_t = jnp.pad(jnp.swapaxes(lf_all, 1, 2), ((0, 0), (0, 0), (0, kpad - past_len - dec_seq)))
    ys = _sample_attention("fox", qs, ks0, vs0, gs, cache_fox_k, cache_fox_v,
                           _tri(kpad, strict=False, lower=False), lf_all_t)
    xp1 = _out_proj(xp, yp, wo0, tm=512)
    xs1 = _out_proj(xs, ys, wo0, tm=n_s)

    qp, kp1, vp1, gp = _in_proj(xp1, norm_1, w1, tm=512)
    qs, ks1, vs1, gs = _in_proj(xs1, norm_1, w1, tm=n_s)
    yp = _sb_attention(qp, kp1, vp1, gp, _tri(SB_SEG, strict=True, lower=True), batch=batch, seq=seq)
    ys = _sample_attention("sb", qs, ks1, vs1, gs, cache_sb_k, cache_sb_v,
                           _tri(kpad, strict=True, lower=True))
    y_prompt = _out_proj(xp1, yp, wo1, norm_f, tm=512)
    y_sample = _out_proj(xs1, ys, wo1, norm_f, tm=n_s)

    hp = (batch, seq, N_HEADS, HEAD_DIM)
    hs = (dec_batch, dec_seq, N_HEADS, HEAD_DIM)
    return (y_prompt.reshape(batch, seq, D_MODEL), y_sample.reshape(dec_batch, dec_seq, D_MODEL),
            kp0.reshape(hp), vp0.reshape(hp), lfp[:, :, :N_HEADS],
            kp1.reshape(hp), vp1.reshape(hp),
            ks0.reshape(hs), vs0.reshape(hs), lfs,
            ks1.reshape(hs), vs1.reshape(hs))
```

```python
from functools import partial

import jax
import jax.numpy as jnp
from jax import lax
from jax.experimental import pallas as pl
from jax.experimental.pallas import tpu as pltpu

D_MODEL = 2048
N_HEADS = 16
HEAD_DIM = 128
D_INNER = N_HEADS * HEAD_DIM
EPS = 1e-6
SCALE = HEAD_DIM ** -0.5
LOG2E = 1.4426950408889634
NEG = -1e30

LANES = 128
F32 = jnp.float32
BF16 = jnp.bfloat16

VMEM_LIMIT = 56 * 1024 * 1024

ROW_TILE = 256
CAST_ROWS = 256
ATT_T = 512
ATT_RC = 32
SB_SEG = 256
CUM_T = 512
HEAD_GROUP = 4
SAMPLE_KPAD = 128


def _params(n_axes):
    return pltpu.CompilerParams(dimension_semantics=("arbitrary",) * n_axes,
                                vmem_limit_bytes=VMEM_LIMIT)


def _log_sigmoid(z):
    return jnp.minimum(z, 0.0) - jnp.log1p(jnp.exp(-jnp.abs(z)))


def _split3(x):
    x1 = x.astype(BF16)
    r1 = x - x1.astype(F32)
    x2 = r1.astype(BF16)
    x3 = (r1 - x2.astype(F32)).astype(BF16)
    return x1, x2, x3


def _dot(a, b):
    return jnp.dot(a, b, preferred_element_type=F32)


def _dot_nt(a, b):
    return lax.dot_general(a, b, (((1,), (1,)), ((), ())), preferred_element_type=F32)


def _tree(op, xs):
    while len(xs) > 1:
        xs = [op(xs[i], xs[i + 1]) for i in range(0, len(xs) - 1, 2)] + ([xs[-1]] if len(xs) % 2 else [])
    return xs[0]


def _rmsnorm(x, w):
    ms = jnp.mean(x * x, axis=-1, keepdims=True)
    return (x * lax.rsqrt(ms + EPS)) * w


def _two_stream_specs(n_p_tiles, tm, n_s, tail):
    zeros = (0,) * len(tail)
    return (pl.BlockSpec((tm,) + tail, lambda m: (jnp.minimum(m, n_p_tiles - 1),) + zeros),
            pl.BlockSpec((n_s,) + tail, lambda m: (0,) + zeros))


def _resident(shape, index):
    return pl.BlockSpec(shape, lambda m: index, pipeline_mode=pl.Buffered(1))


def _cast_weight(w_ref, wb_scr):
    def cast(i, carry):
        rows = pl.ds(pl.multiple_of(i * CAST_ROWS, CAST_ROWS), CAST_ROWS)
        wb_scr[rows, :] = w_ref[rows, :].astype(BF16)
        return carry
    lax.fori_loop(0, w_ref.shape[0] // CAST_ROWS, cast, 0)


def _norm_kernel(xp_ref, xs_ref, nw_ref, op_ref, os_ref, *, n_p_tiles):
    m = pl.program_id(0)

    @pl.when(m < n_p_tiles)
    def _():
        op_ref[...] = _rmsnorm(xp_ref[...], nw_ref[...]).astype(BF16)

    @pl.when(m == n_p_tiles)
    def _():
        os_ref[...] = _rmsnorm(xs_ref[...], nw_ref[...]).astype(BF16)


def _norm(xp, xs, norm_w):
    n_p, n_s = xp.shape[0], xs.shape[0]
    n_p_tiles = n_p // ROW_TILE
    specs = _two_stream_specs(n_p_tiles, ROW_TILE, n_s, (D_MODEL,))
    return pl.pallas_call(
        partial(_norm_kernel, n_p_tiles=n_p_tiles), grid=(n_p_tiles + 1,),
        in_specs=[*specs, pl.BlockSpec((1, D_MODEL), lambda m: (0, 0))],
        out_specs=list(specs),
        out_shape=[jax.ShapeDtypeStruct((n_p, D_MODEL), BF16),
                   jax.ShapeDtypeStruct((n_s, D_MODEL), BF16)],
        compiler_params=_params(1), name="rmsnorm_in",
    )(xp, xs, norm_w.reshape(1, D_MODEL))


def _proj_kernel(*refs, kind, with_f, n_p_tiles):
    xp_ref, xs_ref, w_ref = refs[:3]
    refs = refs[3:]
    if with_f:
        wf_ref, bf_ref = refs[:2]
        refs = refs[2:]
    n_out = {"q": 2 if with_f else 1, "kv": 2, "g": 1}[kind]
    outs_p, outs_s = refs[:n_out], refs[n_out:2 * n_out]
    scr = refs[2 * n_out:]
    wb_scr = scr[0]
    m = pl.program_id(0)

    @pl.when(m == 0)
    def _():
        _cast_weight(w_ref, wb_scr)
        if with_f:
            scr[1][...] = jnp.zeros(scr[1].shape, BF16)
            scr[1][:, 0:N_HEADS] = wf_ref[...].astype(BF16)

    def run(x_ref, outs):
        x = x_ref[...]
        acc = _dot(x, wb_scr[...])
        if kind == "q":
            outs[0][...] = (acc * (SCALE * LOG2E)).astype(BF16)
            if with_f:
                outs[1][...] = _log_sigmoid(_dot(x, scr[1][...]) + bf_ref[...])
        elif kind == "kv":
            outs[0][...] = acc.reshape(acc.shape[0], N_HEADS, HEAD_DIM)
            outs[1][...] = acc.astype(BF16)
        else:
            outs[0][...] = (acc * jax.nn.sigmoid(acc)).astype(BF16)

    @pl.when(m < n_p_tiles)
    def _():
        run(xp_ref, outs_p)

    @pl.when(m == n_p_tiles)
    def _():
        run(xs_ref, outs_s)


def _proj(xnp, xns, w, group, kind, wf=None, bf=None):
    n_p, n_s = xnp.shape[0], xns.shape[0]
    n_p_tiles = n_p // ROW_TILE
    with_f = wf is not None
    flat = _two_stream_specs(n_p_tiles, ROW_TILE, n_s, (D_INNER,))
    in_specs = [*flat, _resident((D_MODEL, D_INNER), (0, group))]
    args = [xnp, xns, w]
    scratch = [pltpu.VMEM((D_MODEL, D_INNER), BF16)]
    if with_f:
        in_specs += [_resident((D_MODEL, N_HEADS), (0, 0)), _resident((1, LANES), (0, 0))]
        args += [wf, bf]
        scratch.append(pltpu.VMEM((D_MODEL, LANES), BF16))

    def shapes(tail, dtype):
        return [jax.ShapeDtypeStruct((n,) + tail, dtype) for n in (n_p, n_s)]

    if kind == "kv":
        heads = _two_stream_specs(n_p_tiles, ROW_TILE, n_s, (N_HEADS, HEAD_DIM))
        out_specs = [heads[0], flat[0], heads[1], flat[1]]
        s4, s2 = shapes((N_HEADS, HEAD_DIM), F32), shapes((D_INNER,), BF16)
        out_shape = [s4[0], s2[0], s4[1], s2[1]]
    elif with_f:
        lanes = _two_stream_specs(n_p_tiles, ROW_TILE, n_s, (LANES,))
        out_specs = [flat[0], lanes[0], flat[1], lanes[1]]
        s2, sl = shapes((D_INNER,), BF16), shapes((LANES,), F32)
        out_shape = [s2[0], sl[0], s2[1], sl[1]]
    else:
        out_specs = list(flat)
        out_shape = shapes((D_INNER,), BF16)
    return pl.pallas_call(
        partial(_proj_kernel, kind=kind, with_f=with_f, n_p_tiles=n_p_tiles),
        grid=(n_p_tiles + 1,), in_specs=in_specs, out_specs=out_specs, out_shape=out_shape,
        scratch_shapes=scratch, compiler_params=_params(1), name="proj_" + kind,
    )(*args)


def _out_proj_kernel(*refs, final, n_p_tiles):
    xp_ref, xs_ref, yp_ref, ys_ref, w_ref, nw_ref = refs[:6]
    n_out = 1 if final else 2
    outs_p, outs_s = refs[6:6 + n_out], refs[6 + n_out:6 + 2 * n_out]
    wb_scr = refs[6 + 2 * n_out]
    m = pl.program_id(0)

    @pl.when(m == 0)
    def _():
        _cast_weight(w_ref, wb_scr)

    def run(x_ref, y_ref, outs):
        r = x_ref[...] + _dot(y_ref[...], wb_scr[...])
        rn = _rmsnorm(r, nw_ref[...])
        if final:
            outs[0][...] = rn
        else:
            outs[0][...] = r
            outs[1][...] = rn.astype(BF16)

    @pl.when(m < n_p_tiles)
    def _():
        run(xp_ref, yp_ref, outs_p)

    @pl.when(m == n_p_tiles)
    def _():
        run(xs_ref, ys_ref, outs_s)


def _out_proj(xp, xs, yp, ys, w, norm_w, *, final):
    n_p, n_s = xp.shape[0], xs.shape[0]
    n_p_tiles = n_p // ROW_TILE
    flat = _two_stream_specs(n_p_tiles, ROW_TILE, n_s, (D_MODEL,))
    f32s = [jax.ShapeDtypeStruct((n, D_MODEL), F32) for n in (n_p, n_s)]
    bf16s = [jax.ShapeDtypeStruct((n, D_MODEL), BF16) for n in (n_p, n_s)]
    if final:
        out_specs, out_shape = list(flat), f32s
    else:
        out_specs = [flat[0], flat[0], flat[1], flat[1]]
        out_shape = [f32s[0], bf16s[0], f32s[1], bf16s[1]]
    return pl.pallas_call(
        partial(_out_proj_kernel, final=final, n_p_tiles=n_p_tiles), grid=(n_p_tiles + 1,),
        in_specs=[*flat, *flat, _resident((D_INNER, D_MODEL), (0, 0)),
                  pl.BlockSpec((1, D_MODEL), lambda m: (0, 0))],
        out_specs=out_specs, out_shape=out_shape,
        scratch_shapes=[pltpu.VMEM((D_INNER, D_MODEL), BF16)],
        compiler_params=_params(1), name="out_proj_final" if final else "out_proj",
    )(xp, xs, yp, ys, w, norm_w.reshape(1, D_MODEL))


def _cumsum_kernel(lf_ref, tri_ref, c_ref):
    n_chunks = lf_ref.shape[0] // CUM_T
    tri = tri_ref[...]
    carry = jnp.zeros((1, LANES), F32)
    for i in range(n_chunks):
        x1, x2, x3 = _split3(lf_ref[i * CUM_T:(i + 1) * CUM_T, :])
        c = (_dot(tri, x1) + _dot(tri, x2)) + _dot(tri, x3) + carry
        c_ref[i * CUM_T:(i + 1) * CUM_T, :] = c
        carry = c[CUM_T - 1:CUM_T, :]


def _cumsum(lf, tri_incl):
    b, t, _ = lf.shape
    return pl.pallas_call(
        _cumsum_kernel, grid=(b,),
        in_specs=[pl.BlockSpec((None, t, LANES), lambda i: (i, 0, 0)),
                  pl.BlockSpec((CUM_T, CUM_T), lambda i: (0, 0))],
        out_specs=pl.BlockSpec((None, t, LANES), lambda i: (i, 0, 0)),
        out_shape=jax.ShapeDtypeStruct((b, t, LANES), F32),
        compiler_params=_params(1), name="logf_cumsum",
    )(lf, tri_incl)


def _decay_columns(c_tile, h, for_keys):
    n = c_tile.shape[0]
    lane = lax.broadcasted_iota(jnp.int32, (n, LANES), 1)
    c = jnp.sum(jnp.where(lane == h, c_tile, 0.0), axis=1, keepdims=True) * LOG2E
    if for_keys:
        c = -c
    first_c, first_one = (3, 0) if for_keys else (0, 3)
    out = jnp.where((lane >= first_one) & (lane < first_one + 3), 1.0, 0.0)
    for i, part in enumerate(_split3(c)):
        out = jnp.where(lane == first_c + i, part.astype(F32), out)
    return out.astype(BF16)


def _fox_kernel(q_ref, k_ref, v_ref, g_ref, cq_ref, ck_ref, y_ref,
                kb_scr, qa_scr, s_scr, p_scr, m_scr, l_scr, acc_scr):
    h = pl.program_id(1)
    qi = pl.program_id(2)
    t = ATT_T
    seq = k_ref.shape[0]

    @pl.when(qi == 0)
    def _():
        def fill(i, carry):
            rows = pl.ds(pl.multiple_of(i * t, t), t)
            kb_scr[rows, 0:HEAD_DIM] = k_ref[rows, :]
            kb_scr[rows, HEAD_DIM:] = _decay_columns(ck_ref[rows, :], h, True)
            return carry
        lax.fori_loop(0, seq // t, fill, 0)

    qa_scr[:, 0:HEAD_DIM] = q_ref[...]
    qa_scr[:, HEAD_DIM:] = _decay_columns(cq_ref[...], h, False)
    m_scr[...] = jnp.full((t, LANES), NEG, F32)
    l_scr[...] = jnp.zeros((t, LANES), F32)
    acc_scr[...] = jnp.zeros((t, HEAD_DIM), F32)

    def scores(j, slot):
        off = pl.multiple_of(j * t, t)
        s_scr[slot] = _dot_nt(qa_scr[...], kb_scr[pl.ds(off, t), :])

    def softmax(slot, masked):
        for r in range(0, t, ATT_RC):
            rows = slice(r, r + ATT_RC)
            s = s_scr[slot, rows, :]
            if masked:
                row = lax.broadcasted_iota(jnp.int32, (ATT_RC, t), 0) + r
                col = lax.broadcasted_iota(jnp.int32, (ATT_RC, t), 1)
                s = jnp.where(col <= row, s, NEG)
            slabs = [s[:, c:c + LANES] for c in range(0, t, LANES)]
            m_prev = m_scr[rows, :]
            m_new = jnp.maximum(m_prev, jnp.max(_tree(jnp.maximum, slabs), axis=1, keepdims=True))
            alpha = jnp.exp2(m_prev - m_new)
            ps = [jnp.exp2(sl - m_new) for sl in slabs]
            l_scr[rows, :] = alpha * l_scr[rows, :] + _tree(jnp.add, ps)
            m_scr[rows, :] = m_new
            acc_scr[rows, :] = alpha * acc_scr[rows, :]
            p_scr[rows, :] = jnp.concatenate(ps, axis=1).astype(BF16)

    def weighted_values(j):
        off = pl.multiple_of(j * t, t)
        acc_scr[...] += _dot(p_scr[...], v_ref[pl.ds(off, t), :])

    scores(0, 0)

    def body(jj, carry):
        a = 2 * jj
        scores(a + 1, 1)
        softmax(0, False)
        weighted_values(a)
        scores(a + 2, 0)
        softmax(1, False)
        weighted_values(a + 1)
        return carry

    lax.fori_loop(0, qi // 2, body, 0)

    @pl.when(qi % 2 == 0)
    def _():
        softmax(0, True)
        weighted_values(qi)

    @pl.when(qi % 2 == 1)
    def _():
        scores(qi, 1)
        softmax(0, False)
        weighted_values(qi - 1)
        softmax(1, True)
        weighted_values(qi)

    o = acc_scr[...] / jnp.sum(l_scr[...], axis=1, keepdims=True)
    y_ref[...] = (o * g_ref[...].astype(F32)).astype(BF16)


def _fox_attention(q, k, v, g, c, *, batch, seq):
    nq = seq // ATT_T
    tok_spec = pl.BlockSpec((ATT_T, HEAD_DIM), lambda b, h, i: (b * nq + i, h))
    seq_spec = pl.BlockSpec((seq, HEAD_DIM), lambda b, h, i: (b, h))
    return pl.pallas_call(
        _fox_kernel, grid=(batch, N_HEADS, nq),
        in_specs=[tok_spec, seq_spec, seq_spec, tok_spec,
                  pl.BlockSpec((None, ATT_T, LANES), lambda b, h, i: (b, i, 0)),
                  pl.BlockSpec((None, seq, LANES), lambda b, h, i: (b, 0, 0))],
        out_specs=tok_spec,
        out_shape=jax.ShapeDtypeStruct((batch * seq, D_INNER), BF16),
        scratch_shapes=[pltpu.VMEM((seq, 2 * HEAD_DIM), BF16),
                        pltpu.VMEM((ATT_T, 2 * HEAD_DIM), BF16),
                        pltpu.VMEM((2, ATT_T, ATT_T), F32), pltpu.VMEM((ATT_T, ATT_T), BF16),
                        pltpu.VMEM((ATT_T, LANES), F32), pltpu.VMEM((ATT_T, LANES), F32),
                        pltpu.VMEM((ATT_T, HEAD_DIM), F32)],
        compiler_params=_params(3), name="fox_attention",
    )(q, k, v, g, c, c)


def _log_gates(z):
    lsig = jnp.minimum(z, 0.0) - jnp.log2(1.0 + jnp.exp2(-jnp.abs(z)))
    return lsig, lsig - z


def _sb_kernel(q_ref, k_ref, v_ref, g_ref, tri_ref, y_ref,
               z_scr, cs_scr, u_scr, a_scr, rs_scr, r_scr, acc_scr):
    qi = pl.program_id(2)
    t = ATT_T
    seg = SB_SEG
    n_seg = t // seg

    r_scr[...] = jnp.zeros((t, LANES), F32)
    acc_scr[...] = jnp.zeros((t, HEAD_DIM), F32)

    def scores(j, k):
        off = pl.multiple_of(j * t, t)
        z_scr[k] = _dot_nt(q_ref[...], k_ref[pl.ds(off, t), :])

    def strict_mask(r):
        row = lax.broadcasted_iota(jnp.int32, (ATT_RC, t), 0) + r
        col = lax.broadcasted_iota(jnp.int32, (ATT_RC, t), 1)
        return col < row

    def log_gates(k, masked):
        for r in range(0, t, ATT_RC):
            rows = slice(r, r + ATT_RC)
            lsig, u = _log_gates(z_scr[k, rows, :])
            if masked:
                u = jnp.where(strict_mask(r), u, 0.0)
            z_scr[k, rows, :] = lsig
            u_scr[k, rows, :] = u.astype(BF16)
            for sgi in range(n_seg):
                slabs = [u[:, c:c + LANES] for c in range(sgi * seg, (sgi + 1) * seg, LANES)]
                rs_scr[k, sgi, rows, :] = jnp.broadcast_to(
                    jnp.sum(_tree(jnp.add, slabs), axis=1, keepdims=True), (ATT_RC, LANES))

    def suffix_sums(k):
        for sgi in range(n_seg):
            cols = slice(sgi * seg, (sgi + 1) * seg)
            cs_scr[k, :, cols] = _dot(u_scr[k, :, cols], tri_ref[...])

    def weights(k, masked):
        for r in range(0, t, ATT_RC):
            rows = slice(r, r + ATT_RC)
            base = r_scr[rows, :]
            x = z_scr[k, rows, :] + cs_scr[k, rows, :]
            parts = []
            for sgi in reversed(range(n_seg)):
                for c in reversed(range(sgi * seg, (sgi + 1) * seg, LANES)):
                    parts.append(jnp.exp2(x[:, c:c + LANES] + base))
                base = base + rs_scr[k, sgi, rows, :]
            a = jnp.concatenate(parts[::-1], axis=1)
            if masked:
                a = jnp.where(strict_mask(r), a, 0.0)
            a_scr[k, rows, :] = a.astype(BF16)
            r_scr[rows, :] = base

    def weighted_values(j, k):
        off = pl.multiple_of(j * t, t)
        acc_scr[...] += _dot(a_scr[k], v_ref[pl.ds(off, t), :])

    scores(qi, 0)
    log_gates(0, True)
    suffix_sums(0)
    weights(0, True)
    weighted_values(qi, 0)
    scores(jnp.maximum(qi - 1, 0), 0)

    def body(jj, carry):
        ja = qi - 1 - 2 * jj
        scores(ja - 1, 1)
        log_gates(0, False)
        suffix_sums(0)
        log_gates(1, False)
        suffix_sums(1)
        weights(0, False)
        weighted_values(ja, 0)
        scores(jnp.maximum(ja - 2, 0), 0)
        weights(1, False)
        weighted_values(ja - 1, 1)
        return carry

    lax.fori_loop(0, qi // 2, body, 0)

    @pl.when(qi % 2 == 1)
    def _():
        log_gates(0, False)
        suffix_sums(0)
        weights(0, False)
        weighted_values(0, 0)

    y_ref[...] = (acc_scr[...] * g_ref[...].astype(F32)).astype(BF16)


def _sb_attention(q, k, v, g, tri_strict, *, batch, seq):
    nq = seq // ATT_T
    tok_spec = pl.BlockSpec((ATT_T, HEAD_DIM), lambda b, h, i: (b * nq + i, h))
    seq_spec = pl.BlockSpec((seq, HEAD_DIM), lambda b, h, i: (b, h))
    return pl.pallas_call(
        _sb_kernel, grid=(batch, N_HEADS, nq),
        in_specs=[tok_spec, seq_spec, seq_spec, tok_spec,
                  pl.BlockSpec((SB_SEG, SB_SEG), lambda b, h, i: (0, 0))],
        out_specs=tok_spec,
        out_shape=jax.ShapeDtypeStruct((batch * seq, D_INNER), BF16),
        scratch_shapes=[pltpu.VMEM((2, ATT_T, ATT_T), F32), pltpu.VMEM((2, ATT_T, ATT_T), F32),
                        pltpu.VMEM((2, ATT_T, ATT_T), BF16), pltpu.VMEM((2, ATT_T, ATT_T), BF16),
                        pltpu.VMEM((2, ATT_T // SB_SEG, ATT_T, LANES), F32),
                        pltpu.VMEM((ATT_T, LANES), F32), pltpu.VMEM((ATT_T, HEAD_DIM), F32)],
        compiler_params=_params(3), name="sb_attention",
    )(q, k, v, g, tri_strict)


def _gather_keys(dst, past_ref, new_ref, hh, past_len, n_new):
    sl = slice(hh * HEAD_DIM, (hh + 1) * HEAD_DIM)
    dst[0:past_len, :] = past_ref[:, sl].astype(BF16)
    pad = jnp.zeros((SAMPLE_KPAD - n_new, HEAD_DIM), BF16)
    dst[past_len:past_len + SAMPLE_KPAD, :] = jnp.concatenate([new_ref[:, sl], pad], axis=0)


def _fox_sample_kernel(q_ref, kn_ref, vn_ref, g_ref, kp_ref, vp_ref, lft_ref, tri_ref, y_ref,
                       kb_scr, vb_scr, c_scr, *, past_len, n_new):
    hg = pl.program_id(1)
    kpad = past_len + SAMPLE_KPAD

    @pl.when(hg == 0)
    def _():
        tri = tri_ref[...]
        x1, x2, x3 = _split3(lft_ref[...])
        c_scr[...] = (_dot(x1, tri) + _dot(x2, tri)) + _dot(x3, tri)

    row = lax.broadcasted_iota(jnp.int32, (n_new, kpad), 0)
    col = lax.broadcasted_iota(jnp.int32, (n_new, kpad), 1)
    qpos = row + past_len
    for hh in range(HEAD_GROUP):
        sl = slice(hh * HEAD_DIM, (hh + 1) * HEAD_DIM)
        _gather_keys(kb_scr, kp_ref, kn_ref, hh, past_len, n_new)
        _gather_keys(vb_scr, vp_ref, vn_ref, hh, past_len, n_new)
        ck = c_scr[pl.ds(hg * HEAD_GROUP + hh, 1), :] * LOG2E
        cq = jnp.sum(jnp.where(col == qpos, ck, 0.0), axis=1, keepdims=True)
        s = _dot_nt(q_ref[:, sl], kb_scr[...]) + (cq - ck)
        s = jnp.where(col <= qpos, s, NEG)
        m = jnp.max(s, axis=1, keepdims=True)
        p = jnp.exp2(s - m)
        l = jnp.sum(p, axis=1, keepdims=True)
        o = _dot(p.astype(BF16), vb_scr[...]) / l
        y_ref[:, sl] = (o * g_ref[:, sl].astype(F32)).astype(BF16)


def _sb_sample_kernel(q_ref, kn_ref, vn_ref, g_ref, kp_ref, vp_ref, tri_ref, y_ref,
                      kb_scr, vb_scr, l_scr, u_scr, *, past_len, n_new):
    kpad = past_len + SAMPLE_KPAD
    row = lax.broadcasted_iota(jnp.int32, (n_new, kpad), 0)
    col = lax.broadcasted_iota(jnp.int32, (n_new, kpad), 1)
    mask = col < row + past_len
    for hh in range(HEAD_GROUP):
        sl = slice(hh * HEAD_DIM, (hh + 1) * HEAD_DIM)
        rows = slice(hh * n_new, (hh + 1) * n_new)
        _gather_keys(kb_scr, kp_ref, kn_ref, hh, past_len, n_new)
        _gather_keys(vb_scr.at[hh], vp_ref, vn_ref, hh, past_len, n_new)
        lsig, u = _log_gates(_dot_nt(q_ref[:, sl], kb_scr[...]))
        l_scr[rows, :] = lsig
        u_scr[rows, :] = jnp.where(mask, u, 0.0).astype(BF16)
    after = _dot(u_scr[...], tri_ref[...])
    for hh in range(HEAD_GROUP):
        sl = slice(hh * HEAD_DIM, (hh + 1) * HEAD_DIM)
        rows = slice(hh * n_new, (hh + 1) * n_new)
        a = jnp.where(mask, jnp.exp2(l_scr[rows, :] + after[rows, :]), 0.0)
        o = _dot(a.astype(BF16), vb_scr[hh])
        y_ref[:, sl] = (o * g_ref[:, sl].astype(F32)).astype(BF16)


def _sample_attention(kind, q, k_new, v_new, g, k_past, v_past, tri, lf_all_t=None):
    batch, past_len = k_past.shape[0], k_past.shape[1]
    n_new = q.shape[0] // batch
    kpad = past_len + SAMPLE_KPAD
    gw = HEAD_GROUP * HEAD_DIM
    tok_spec = pl.BlockSpec((n_new, gw), lambda b, hg: (b, hg))
    past_spec = pl.BlockSpec((None, past_len, gw), lambda b, hg: (b, 0, hg))
    tri_spec = pl.BlockSpec((kpad, kpad), lambda b, hg: (0, 0))
    kp = k_past.reshape(batch, past_len, D_INNER)
    vp = v_past.reshape(batch, past_len, D_INNER)
    if kind == "fox":
        body = partial(_fox_sample_kernel, past_len=past_len, n_new=n_new)
        in_specs = [tok_spec, tok_spec, tok_spec, tok_spec, past_spec, past_spec,
                    pl.BlockSpec((None, N_HEADS, kpad), lambda b, hg: (b, 0, 0)), tri_spec]
        args = (q, k_new, v_new, g, kp, vp, lf_all_t, tri)
        scratch = [pltpu.VMEM((kpad, HEAD_DIM), BF16), pltpu.VMEM((kpad, HEAD_DIM), BF16),
                   pltpu.VMEM((N_HEADS, kpad), F32)]
    else:
        body = partial(_sb_sample_kernel, past_len=past_len, n_new=n_new)
        in_specs = [tok_spec, tok_spec, tok_spec, tok_spec, past_spec, past_spec, tri_spec]
        args = (q, k_new, v_new, g, kp, vp, tri)
        scratch = [pltpu.VMEM((kpad, HEAD_DIM), BF16),
                   pltpu.VMEM((HEAD_GROUP, kpad, HEAD_DIM), BF16),
                   pltpu.VMEM((HEAD_GROUP * n_new, kpad), F32),
                   pltpu.VMEM((HEAD_GROUP * n_new, kpad), BF16)]
    return pl.pallas_call(
        body, grid=(batch, N_HEADS // HEAD_GROUP), in_specs=in_specs, out_specs=tok_spec,
        out_shape=jax.ShapeDtypeStruct((batch * n_new, D_INNER), BF16),
        scratch_shapes=scratch, compiler_params=_params(2), name=kind + "_sample_attention",
    )(*args)


def _tri(n, strict, lower):
    r = lax.broadcasted_iota(jnp.int32, (n, n), 0)
    c = lax.broadcasted_iota(jnp.int32, (n, n), 1)
    if lower:
        keep = (r > c) if strict else (r >= c)
    else:
        keep = (r < c) if strict else (r <= c)
    return keep.astype(BF16)


def kernel(x_prompt, x_sample, cache_fox_k, cache_fox_v, cache_fox_logf, cache_sb_k, cache_sb_v,
           norm_0, w_in_0, b_f_0, w_out_0, norm_1, w_in_1, w_out_1, norm_f):
    batch, seq, _ = x_prompt.shape
    dec_batch, dec_seq, _ = x_sample.shape
    past_len = cache_fox_k.shape[1]
    n_p, n_s = batch * seq, dec_batch * dec_seq
    kpad = past_len + SAMPLE_KPAD

    wf = w_in_0[:, 4 * D_INNER:]
    bf = jnp.pad(b_f_0, (0, LANES - N_HEADS)).reshape(1, LANES)
    xp = x_prompt.reshape(n_p, D_MODEL)
    xs = x_sample.reshape(n_s, D_MODEL)

    xnp, xns = _norm(xp, xs, norm_0)
    qp, lfp, qs, lfs = _proj(xnp, xns, w_in_0, 0, "q", wf, bf)
    kp0, kbp, ks0, kbs = _proj(xnp, xns, w_in_0, 1, "kv")
    vp0, vbp, vs0, vbs = _proj(xnp, xns, w_in_0, 2, "kv")
    gp, gs = _proj(xnp, xns, w_in_0, 3, "g")
    lfp = lfp.reshape(batch, seq, LANES)
    cum = _cumsum(lfp, _tri(CUM_T, strict=False, lower=True))
    yp = _fox_attention(qp, kbp, vbp, gp, cum, batch=batch, seq=seq)
    lfs = lfs[:, :N_HEADS].reshape(dec_batch, dec_seq, N_HEADS)
    lf_all = jnp.concatenate([cache_fox_logf, lfs], axis=1)
    lf_all_t = jnp.pad(jnp.swapaxes(lf_all, 1, 2), ((0, 0), (0, 0), (0, kpad - past_len - dec_seq)))
    ys = _sample_attention("fox", qs, kbs, vbs, gs, cache_fox_k, cache_fox_v,
                           _tri(kpad, strict=False, lower=False), lf_all_t)
    xp1, xnp, xs1, xns = _out_proj(xp, xs, yp, ys, w_out_0, norm_1, final=False)

    qp, qs = _proj(xnp, xns, w_in_1, 0, "q")
    kp1, kbp, ks1, kbs = _proj(xnp, xns, w_in_1, 1, "kv")
    vp1, vbp, vs1, vbs = _proj(xnp, xns, w_in_1, 2, "kv")
    gp, gs = _proj(xnp, xns, w_in_1, 3, "g")
    yp = _sb_attention(qp, kbp, vbp, gp, _tri(SB_SEG, strict=True, lower=True), batch=batch, seq=seq)
    ys = _sample_attention("sb", qs, kbs, vbs, gs, cache_sb_k, cache_sb_v,
                           _tri(kpad, strict=True, lower=True))
    y_prompt, y_sample = _out_proj(xp1, xs1, yp, ys, w_out_1, norm_f, final=True)

    hp = (batch, seq, N_HEADS, HEAD_DIM)
    hs = (dec_batch, dec_seq, N_HEADS, HEAD_DIM)
    return (y_prompt.reshape(batch, seq, D_MODEL), y_sample.reshape(dec_batch, dec_seq, D_MODEL),
            kp0.reshape(hp), vp0.reshape(hp), lfp[:, :, :N_HEADS],
            kp1.reshape(hp), vp1.reshape(hp),
            ks0.reshape(hs), vs0.reshape(hs), lfs,
            ks1.reshape(hs), vs1.reshape(hs))
```

```python
from functools import partial

import jax
import jax.numpy as jnp
from jax import lax
from jax.experimental import pallas as pl
from jax.experimental.pallas import tpu as pltpu

D_MODEL = 2048
N_HEADS = 16
HEAD_DIM = 128
D_INNER = N_HEADS * HEAD_DIM
EPS = 1e-6
SCALE = HEAD_DIM ** -0.5
LOG2E = 1.4426950408889634
NEG = -1e30

LANES = 128
F32 = jnp.float32
BF16 = jnp.bfloat16

VMEM_LIMIT = 56 * 1024 * 1024

ROW_TILE = 256
CAST_ROWS = 256
ATT_TK = 512
ATT_TQ = 2 * ATT_TK
ATT_RC = 32
ATT_RB = 256
SB_SEG = 256
CUM_T = 512
HEAD_GROUP = 8
SAMPLE_KPAD = 128


def _params(n_axes):
    return pltpu.CompilerParams(dimension_semantics=("arbitrary",) * n_axes,
                                vmem_limit_bytes=VMEM_LIMIT)


def _log_sigmoid(z):
    return jnp.minimum(z, 0.0) - jnp.log1p(jnp.exp(-jnp.abs(z)))


def _split3(x):
    x1 = x.astype(BF16)
    r1 = x - x1.astype(F32)
    x2 = r1.astype(BF16)
    x3 = (r1 - x2.astype(F32)).astype(BF16)
    return x1, x2, x3


def _dot(a, b):
    return jnp.dot(a, b, preferred_element_type=F32)


def _dot_nt(a, b):
    return lax.dot_general(a, b, (((1,), (1,)), ((), ())), preferred_element_type=F32)


def _tree(op, xs):
    while len(xs) > 1:
        xs = [op(xs[i], xs[i + 1]) for i in range(0, len(xs) - 1, 2)] + ([xs[-1]] if len(xs) % 2 else [])
    return xs[0]


def _causal(first_row, n_cols, strict):
    row = lax.broadcasted_iota(jnp.int32, (ATT_RC, n_cols), 0) + first_row
    col = lax.broadcasted_iota(jnp.int32, (ATT_RC, n_cols), 1)
    return (col < row) if strict else (col <= row)


def _row_blocks(rows):
    return [(r, min(r + ATT_RB, rows[1])) for r in range(rows[0], rows[1], ATT_RB)]


def _rmsnorm(x, w):
    ms = jnp.mean(x * x, axis=-1, keepdims=True)
    return (x * lax.rsqrt(ms + EPS)) * w


def _two_stream_specs(n_p_tiles, tm, n_s, tail):
    zeros = (0,) * len(tail)
    return (pl.BlockSpec((tm,) + tail, lambda m: (jnp.minimum(m, n_p_tiles - 1),) + zeros),
            pl.BlockSpec((n_s,) + tail, lambda m: (0,) + zeros))


def _resident(shape, index):
    return pl.BlockSpec(shape, lambda m: index, pipeline_mode=pl.Buffered(1))


def _cast_weight(w_ref, wb_scr, transposed=False):
    if transposed:
        for c in range(0, w_ref.shape[0], CAST_ROWS):
            wb_scr[:, c:c + CAST_ROWS] = w_ref[c:c + CAST_ROWS, :].T.astype(BF16)
        return

    def cast(i, carry):
        rows = pl.ds(pl.multiple_of(i * CAST_ROWS, CAST_ROWS), CAST_ROWS)
        wb_scr[rows, :] = w_ref[rows, :].astype(BF16)
        return carry
    lax.fori_loop(0, w_ref.shape[0] // CAST_ROWS, cast, 0)


def _norm_kernel(xp_ref, xs_ref, nw_ref, op_ref, os_ref, *, n_p_tiles):
    m = pl.program_id(0)

    @pl.when(m < n_p_tiles)
    def _():
        op_ref[...] = _rmsnorm(xp_ref[...], nw_ref[...]).astype(BF16)

    @pl.when(m == n_p_tiles)
    def _():
        os_ref[...] = _rmsnorm(xs_ref[...], nw_ref[...]).astype(BF16)


def _norm(xp, xs, norm_w):
    n_p, n_s = xp.shape[0], xs.shape[0]
    n_p_tiles = n_p // ROW_TILE
    specs = _two_stream_specs(n_p_tiles, ROW_TILE, n_s, (D_MODEL,))
    return pl.pallas_call(
        partial(_norm_kernel, n_p_tiles=n_p_tiles), grid=(n_p_tiles + 1,),
        in_specs=[*specs, pl.BlockSpec((1, D_MODEL), lambda m: (0, 0))],
        out_specs=list(specs),
        out_shape=[jax.ShapeDtypeStruct((n_p, D_MODEL), BF16),
                   jax.ShapeDtypeStruct((n_s, D_MODEL), BF16)],
        compiler_params=_params(1), name="rmsnorm_in",
    )(xp, xs, norm_w.reshape(1, D_MODEL))


def _proj_kernel(*refs, kind, with_f, transposed, n_p_tiles):
    xp_ref, xs_ref, w_ref = refs[:3]
    refs = refs[3:]
    if with_f:
        wf_ref, bf_ref = refs[:2]
        refs = refs[2:]
    n_out = {"q": 2 if with_f else 1, "kv": 2, "g": 1}[kind]
    outs_p, outs_s = refs[:n_out], refs[n_out:2 * n_out]
    scr = refs[2 * n_out:]
    wb_scr = scr[0]
    m = pl.program_id(0)

    @pl.when(m == 0)
    def _():
        _cast_weight(w_ref, wb_scr, transposed)
        if with_f:
            scr[1][...] = jnp.zeros(scr[1].shape, BF16)
            scr[1][0:N_HEADS, :] = wf_ref[...].astype(BF16)

    def run(x_ref, outs):
        x = x_ref[...]
        acc = _dot(x, wb_scr[...])
        if kind == "q":
            outs[0][...] = (acc * (SCALE * LOG2E)).astype(BF16)
            if with_f:
                outs[1][...] = _log_sigmoid(_dot_nt(x, scr[1][...]) + bf_ref[...])
        elif kind == "kv":
            outs[0][...] = acc.reshape(acc.shape[0], N_HEADS, HEAD_DIM)
            outs[1][...] = acc.astype(BF16)
        else:
            outs[0][...] = (acc * jax.nn.sigmoid(acc)).astype(BF16)

    @pl.when(m < n_p_tiles)
    def _():
        run(xp_ref, outs_p)

    @pl.when(m == n_p_tiles)
    def _():
        run(xs_ref, outs_s)


def _proj(xnp, xns, w, group, kind, bf=None, *, transposed=False):
    n_p, n_s = xnp.shape[0], xns.shape[0]
    n_p_tiles = n_p // ROW_TILE
    with_f = bf is not None
    assert transposed or not with_f
    flat = _two_stream_specs(n_p_tiles, ROW_TILE, n_s, (D_INNER,))
    if transposed:
        w_spec = _resident((D_INNER, D_MODEL), (group, 0))
    else:
        w_spec = _resident((D_MODEL, D_INNER), (0, group))
    in_specs = [*flat, w_spec]
    args = [xnp, xns, w]
    scratch = [pltpu.VMEM((D_MODEL, D_INNER), BF16)]
    if with_f:
        in_specs += [_resident((N_HEADS, D_MODEL), (4 * D_INNER // N_HEADS, 0)),
                     _resident((1, LANES), (0, 0))]
        args += [w, bf]
        scratch.append(pltpu.VMEM((LANES, D_MODEL), BF16))

    def shapes(tail, dtype):
        return [jax.ShapeDtypeStruct((n,) + tail, dtype) for n in (n_p, n_s)]

    if kind == "kv":
        heads = _two_stream_specs(n_p_tiles, ROW_TILE, n_s, (N_HEADS, HEAD_DIM))
        out_specs = [heads[0], flat[0], heads[1], flat[1]]
        s4, s2 = shapes((N_HEADS, HEAD_DIM), F32), shapes((D_INNER,), BF16)
        out_shape = [s4[0], s2[0], s4[1], s2[1]]
    elif with_f:
        lanes = _two_stream_specs(n_p_tiles, ROW_TILE, n_s, (LANES,))
        out_specs = [flat[0], lanes[0], flat[1], lanes[1]]
        s2, sl = shapes((D_INNER,), BF16), shapes((LANES,), F32)
        out_shape = [s2[0], sl[0], s2[1], sl[1]]
    else:
        out_specs = list(flat)
        out_shape = shapes((D_INNER,), BF16)
    return pl.pallas_call(
        partial(_proj_kernel, kind=kind, with_f=with_f, transposed=transposed,
                n_p_tiles=n_p_tiles),
        grid=(n_p_tiles + 1,), in_specs=in_specs, out_specs=out_specs, out_shape=out_shape,
        scratch_shapes=scratch, compiler_params=_params(1), name="proj_" + kind,
    )(*args)


def _out_proj_kernel(*refs, final, n_p_tiles):
    xp_ref, xs_ref, yp_ref, ys_ref, w_ref, nw_ref = refs[:6]
    n_out = 1 if final else 2
    outs_p, outs_s = refs[6:6 + n_out], refs[6 + n_out:6 + 2 * n_out]
    wb_scr = refs[6 + 2 * n_out]
    m = pl.program_id(0)

    @pl.when(m == 0)
    def _():
        _cast_weight(w_ref, wb_scr)

    def run(x_ref, y_ref, outs):
        r = x_ref[...] + _dot(y_ref[...], wb_scr[...])
        rn = _rmsnorm(r, nw_ref[...])
        if final:
            outs[0][...] = rn
        else:
            outs[0][...] = r
            outs[1][...] = rn.astype(BF16)

    @pl.when(m < n_p_tiles)
    def _():
        run(xp_ref, yp_ref, outs_p)

    @pl.when(m == n_p_tiles)
    def _():
        run(xs_ref, ys_ref, outs_s)


def _out_proj(xp, xs, yp, ys, w, norm_w, *, final):
    n_p, n_s = xp.shape[0], xs.shape[0]
    n_p_tiles = n_p // ROW_TILE
    flat = _two_stream_specs(n_p_tiles, ROW_TILE, n_s, (D_MODEL,))
    f32s = [jax.ShapeDtypeStruct((n, D_MODEL), F32) for n in (n_p, n_s)]
    bf16s = [jax.ShapeDtypeStruct((n, D_MODEL), BF16) for n in (n_p, n_s)]
    if final:
        out_specs, out_shape = list(flat), f32s
    else:
        out_specs = [flat[0], flat[0], flat[1], flat[1]]
        out_shape = [f32s[0], bf16s[0], f32s[1], bf16s[1]]
    return pl.pallas_call(
        partial(_out_proj_kernel, final=final, n_p_tiles=n_p_tiles), grid=(n_p_tiles + 1,),
        in_specs=[*flat, *flat, _resident((D_INNER, D_MODEL), (0, 0)),
                  pl.BlockSpec((1, D_MODEL), lambda m: (0, 0))],
        out_specs=out_specs, out_shape=out_shape,
        scratch_shapes=[pltpu.VMEM((D_INNER, D_MODEL), BF16)],
        compiler_params=_params(1), name="out_proj_final" if final else "out_proj",
    )(xp, xs, yp, ys, w, norm_w.reshape(1, D_MODEL))


def _cumsum_kernel(lf_ref, tri_ref, c_ref):
    n_chunks = lf_ref.shape[0] // CUM_T
    tri = tri_ref[...]
    carry = jnp.zeros((1, LANES), F32)
    for i in range(n_chunks):
        x1, x2, x3 = _split3(lf_ref[i * CUM_T:(i + 1) * CUM_T, :])
        c = (_dot(tri, x1) + _dot(tri, x2)) + _dot(tri, x3) + carry
        c_ref[i * CUM_T:(i + 1) * CUM_T, :] = c
        carry = c[CUM_T - 1:CUM_T, :]


def _cumsum(lf, tri_incl):
    b, t, _ = lf.shape
    return pl.pallas_call(
        _cumsum_kernel, grid=(b,),
        in_specs=[pl.BlockSpec((None, t, LANES), lambda i: (i, 0, 0)),
                  pl.BlockSpec((CUM_T, CUM_T), lambda i: (0, 0))],
        out_specs=pl.BlockSpec((None, t, LANES), lambda i: (i, 0, 0)),
        out_shape=jax.ShapeDtypeStruct((b, t, LANES), F32),
        compiler_params=_params(1), name="logf_cumsum",
    )(lf, tri_incl)


def _decay_columns(c_tile, h, for_keys):
    n = c_tile.shape[0]
    lane = lax.broadcasted_iota(jnp.int32, (n, LANES), 1)
    c = jnp.sum(jnp.where(lane == h, c_tile, 0.0), axis=1, keepdims=True) * LOG2E
    if for_keys:
        c = -c
    first_c, first_one = (3, 0) if for_keys else (0, 3)
    out = jnp.where((lane >= first_one) & (lane < first_one + 3), 1.0, 0.0)
    for i, part in enumerate(_split3(c)):
        out = jnp.where(lane == first_c + i, part.astype(F32), out)
    return out.astype(BF16)


def _fox_kernel(q_ref, k_ref, v_ref, g_ref, cq_ref, ck_ref, y_ref,
                kb_scr, qa_scr, s_scr, p_scr, m_scr, l_scr, acc_scr):
    h = pl.program_id(1)
    qi = pl.program_id(2)
    tq, tk = ATT_TQ, ATT_TK
    seq = k_ref.shape[0]
    all_rows, hi_rows = (0, tq), (tk, tq)

    @pl.when(qi == 0)
    def _():
        def fill(i, carry):
            rows = pl.ds(pl.multiple_of(i * tk, tk), tk)
            kb_scr[rows, 0:HEAD_DIM] = k_ref[rows, :]
            kb_scr[rows, HEAD_DIM:] = _decay_columns(ck_ref[rows, :], h, True)
            return carry
        lax.fori_loop(0, seq // tk, fill, 0)

    qa_scr[:, 0:HEAD_DIM] = q_ref[...]
    qa_scr[:, HEAD_DIM:] = _decay_columns(cq_ref[...], h, False)
    m_scr[...] = jnp.full((tq, LANES), NEG, F32)
    l_scr[...] = jnp.zeros((tq, LANES), F32)
    acc_scr[...] = jnp.zeros((tq, HEAD_DIM), F32)

    def scores(j, slot, rows):
        off = pl.multiple_of(j * tk, tk)
        s_scr[slot, rows[0]:rows[1], :] = _dot_nt(qa_scr[rows[0]:rows[1], :], kb_scr[pl.ds(off, tk), :])

    def softmax(slot, rows, diag_rows=None):
        for r in range(rows[0], rows[1], ATT_RC):
            sl = slice(r, r + ATT_RC)
            s = s_scr[slot, sl, :]
            if diag_rows is not None and diag_rows[0] <= r < diag_rows[1]:
                s = jnp.where(_causal(r - diag_rows[0], tk, strict=False), s, NEG)
            slabs = [s[:, c:c + LANES] for c in range(0, tk, LANES)]
            m_prev = m_scr[sl, :]
            m_new = jnp.maximum(m_prev, jnp.max(_tree(jnp.maximum, slabs), axis=1, keepdims=True))
            alpha = jnp.exp2(m_prev - m_new)
            ps = [jnp.exp2(x - m_new) for x in slabs]
            l_scr[sl, :] = alpha * l_scr[sl, :] + _tree(jnp.add, ps)
            m_scr[sl, :] = m_new
            acc_scr[sl, :] = alpha * acc_scr[sl, :]
            p_scr[slot, sl, :] = jnp.concatenate(ps, axis=1).astype(BF16)

    def weighted_values(j, slot, rows):
        off = pl.multiple_of(j * tk, tk)
        sl = slice(rows[0], rows[1])
        acc_scr[sl, :] += _dot(p_scr[slot, sl, :], v_ref[pl.ds(off, tk), :])

    def attend(j, slot, rows, diag_rows=None):
        for blk in _row_blocks(rows):
            softmax(slot, blk, diag_rows)
            weighted_values(j, slot, blk)

    scores(0, 0, all_rows)

    def body(jj, carry):
        a = 2 * jj
        scores(a + 1, 1, all_rows)
        attend(a, 0, all_rows)
        scores(a + 2, 0, all_rows)
        attend(a + 1, 1, all_rows)
        return carry

    lax.fori_loop(0, qi, body, 0)

    d = 2 * qi
    scores(d + 1, 1, hi_rows)
    attend(d, 0, all_rows, diag_rows=(0, tk))
    attend(d + 1, 1, hi_rows, diag_rows=hi_rows)

    o = acc_scr[...] / jnp.sum(l_scr[...], axis=1, keepdims=True)
    y_ref[...] = (o * g_ref[...].astype(F32)).astype(BF16)


def _fox_attention(q, k, v, g, c, *, batch, seq):
    nq = seq // ATT_TQ
    tok_spec = pl.BlockSpec((ATT_TQ, HEAD_DIM), lambda b, h, i: (b * nq + i, h))
    seq_spec = pl.BlockSpec((seq, HEAD_DIM), lambda b, h, i: (b, h))
    return pl.pallas_call(
        _fox_kernel, grid=(batch, N_HEADS, nq),
        in_specs=[tok_spec, seq_spec, seq_spec, tok_spec,
                  pl.BlockSpec((None, ATT_TQ, LANES), lambda b, h, i: (b, i, 0)),
                  pl.BlockSpec((None, seq, LANES), lambda b, h, i: (b, 0, 0))],
        out_specs=tok_spec,
        out_shape=jax.ShapeDtypeStruct((batch * seq, D_INNER), BF16),
        scratch_shapes=[pltpu.VMEM((seq, 2 * HEAD_DIM), BF16),
                        pltpu.VMEM((ATT_TQ, 2 * HEAD_DIM), BF16),
                        pltpu.VMEM((2, ATT_TQ, ATT_TK), F32), pltpu.VMEM((2, ATT_TQ, ATT_TK), BF16),
                        pltpu.VMEM((ATT_TQ, LANES), F32), pltpu.VMEM((ATT_TQ, LANES), F32),
                        pltpu.VMEM((ATT_TQ, HEAD_DIM), F32)],
        compiler_params=_params(3), name="fox_attention",
    )(q, k, v, g, c, c)


def _log_gates(z):
    lsig = jnp.minimum(z, 0.0) - jnp.log2(1.0 + jnp.exp2(-jnp.abs(z)))
    return lsig, lsig - z


def _sb_kernel(q_ref, k_ref, v_ref, g_ref, tri_ref, y_ref,
               z_scr, l_scr, cs_scr, u_scr, a_scr, rs_scr, r_scr, acc_scr):
    qi = pl.program_id(2)
    tq, tk = ATT_TQ, ATT_TK
    seg = SB_SEG
    n_seg = tk // seg
    all_rows, lo_rows, hi_rows = (0, tq), (0, tk), (tk, tq)

    r_scr[...] = jnp.zeros((tq, LANES), F32)
    acc_scr[...] = jnp.zeros((tq, HEAD_DIM), F32)

    def on_diag(r, diag_rows):
        return diag_rows is not None and diag_rows[0] <= r < diag_rows[1]

    def scores(j, k, rows):
        off = pl.multiple_of(j * tk, tk)
        z_scr[k, rows[0]:rows[1], :] = _dot_nt(q_ref[rows[0]:rows[1], :], k_ref[pl.ds(off, tk), :])

    def log_gates(k, rows, diag_rows=None):
        for r in range(rows[0], rows[1], ATT_RC):
            sl = slice(r, r + ATT_RC)
            lsig, u = _log_gates(z_scr[k, sl, :])
            if on_diag(r, diag_rows):
                u = jnp.where(_causal(r - diag_rows[0], tk, strict=True), u, 0.0)
            l_scr[k, sl, :] = lsig
            u_scr[k, sl, :] = u.astype(BF16)
            for sgi in range(n_seg):
                slabs = [u[:, c:c + LANES] for c in range(sgi * seg, (sgi + 1) * seg, LANES)]
                rs_scr[k, sgi, sl, :] = jnp.broadcast_to(
                    jnp.sum(_tree(jnp.add, slabs), axis=1, keepdims=True), (ATT_RC, LANES))

    def suffix_sums(k, rows):
        sl = slice(rows[0], rows[1])
        for sgi in range(n_seg):
            cols = slice(sgi * seg, (sgi + 1) * seg)
            cs_scr[k, sl, cols] = _dot(u_scr[k, sl, cols], tri_ref[...])

    def weights(k, rows, diag_rows=None):
        for r in range(rows[0], rows[1], ATT_RC):
            sl = slice(r, r + ATT_RC)
            base = r_scr[sl, :]
            x = l_scr[k, sl, :] + cs_scr[k, sl, :]
            parts = []
            for sgi in reversed(range(n_seg)):
                for c in reversed(range(sgi * seg, (sgi + 1) * seg, LANES)):
                    parts.append(jnp.exp2(x[:, c:c + LANES] + base))
                base = base + rs_scr[k, sgi, sl, :]
            a = jnp.concatenate(parts[::-1], axis=1)
            if on_diag(r, diag_rows):
                a = jnp.where(_causal(r - diag_rows[0], tk, strict=True), a, 0.0)
            a_scr[k, sl, :] = a.astype(BF16)
            r_scr[sl, :] = base

    def weighted_values(j, k, rows):
        off = pl.multiple_of(j * tk, tk)
        sl = slice(rows[0], rows[1])
        acc_scr[sl, :] += _dot(a_scr[k, sl, :], v_ref[pl.ds(off, tk), :])

    d = 2 * qi
    def gates(k, rows, diag_rows=None):
        for blk in _row_blocks(rows):
            log_gates(k, blk, diag_rows)
            suffix_sums(k, blk)

    def finish(j, k, rows, diag_rows=None):
        for blk in _row_blocks(rows):
            weights(k, blk, diag_rows)
            weighted_values(j, k, blk)

    scores(d + 1, 1, hi_rows)
    scores(d, 0, all_rows)
    gates(1, hi_rows, diag_rows=hi_rows)
    gates(0, all_rows, diag_rows=lo_rows)
    finish(d + 1, 1, hi_rows, diag_rows=hi_rows)
    finish(d, 0, all_rows, diag_rows=lo_rows)

    @pl.when(qi > 0)
    def _():
        scores(d - 1, 1, all_rows)
        scores(d - 2, 0, all_rows)
        gates(1, all_rows)
        gates(0, all_rows)

    def body(jj, carry):
        ja = d - 1 - 2 * jj
        scores(ja - 2, 1, all_rows)
        scores(ja - 3, 0, all_rows)
        for k, j in ((1, ja), (0, ja - 1)):
            for blk in _row_blocks(all_rows):
                finish(j, k, blk)
                gates(k, blk)
        return carry

    lax.fori_loop(0, qi - 1, body, 0)

    @pl.when(qi > 0)
    def _():
        finish(1, 1, all_rows)
        finish(0, 0, all_rows)

    y_ref[...] = (acc_scr[...] * g_ref[...].astype(F32)).astype(BF16)


def _sb_attention(q, k, v, g, tri_strict, *, batch, seq):
    nq = seq // ATT_TQ
    tok_spec = pl.BlockSpec((ATT_TQ, HEAD_DIM), lambda b, h, i: (b * nq + i, h))
    seq_spec = pl.BlockSpec((seq, HEAD_DIM), lambda b, h, i: (b, h))
    tile_f32 = pltpu.VMEM((2, ATT_TQ, ATT_TK), F32)
    tile_bf16 = pltpu.VMEM((2, ATT_TQ, ATT_TK), BF16)
    return pl.pallas_call(
        _sb_kernel, grid=(batch, N_HEADS, nq),
        in_specs=[tok_spec, seq_spec, seq_spec, tok_spec,
                  pl.BlockSpec((SB_SEG, SB_SEG), lambda b, h, i: (0, 0))],
        out_specs=tok_spec,
        out_shape=jax.ShapeDtypeStruct((batch * seq, D_INNER), BF16),
        scratch_shapes=[tile_f32, tile_f32, tile_f32, tile_bf16, tile_bf16,
                        pltpu.VMEM((2, ATT_TK // SB_SEG, ATT_TQ, LANES), F32),
                        pltpu.VMEM((ATT_TQ, LANES), F32), pltpu.VMEM((ATT_TQ, HEAD_DIM), F32)],
        compiler_params=_params(3), name="sb_attention",
    )(q, k, v, g, tri_strict)


def _heads_on_lanes(past_ref):
    x = past_ref[...]
    return x.reshape(x.shape[0], x.shape[1] * x.shape[2])


def _gather_keys(dst, past, new_ref, hh, past_len, n_new):
    sl = slice(hh * HEAD_DIM, (hh + 1) * HEAD_DIM)
    dst[0:past_len, :] = past[:, sl].astype(BF16)
    pad = jnp.zeros((SAMPLE_KPAD - n_new, HEAD_DIM), BF16)
    dst[past_len:past_len + SAMPLE_KPAD, :] = jnp.concatenate([new_ref[:, sl], pad], axis=0)


def _fox_sample_kernel(q_ref, kn_ref, vn_ref, g_ref, kp_ref, vp_ref, lft_ref, tri_ref, y_ref,
                       kb_scr, vb_scr, c_scr, *, past_len, n_new):
    hg = pl.program_id(1)
    kpad = past_len + SAMPLE_KPAD

    @pl.when(hg == 0)
    def _():
        tri = tri_ref[...]
        x1, x2, x3 = _split3(lft_ref[...])
        c_scr[...] = (_dot(x1, tri) + _dot(x2, tri)) + _dot(x3, tri)

    row = lax.broadcasted_iota(jnp.int32, (n_new, kpad), 0)
    col = lax.broadcasted_iota(jnp.int32, (n_new, kpad), 1)
    qpos = row + past_len
    k_past, v_past = _heads_on_lanes(kp_ref), _heads_on_lanes(vp_ref)
    for hh in range(HEAD_GROUP):
        sl = slice(hh * HEAD_DIM, (hh + 1) * HEAD_DIM)
        _gather_keys(kb_scr, k_past, kn_ref, hh, past_len, n_new)
        _gather_keys(vb_scr, v_past, vn_ref, hh, past_len, n_new)
        ck = c_scr[pl.ds(hg * HEAD_GROUP + hh, 1), :] * LOG2E
        cq = jnp.sum(jnp.where(col == qpos, ck, 0.0), axis=1, keepdims=True)
        s = _dot_nt(q_ref[:, sl], kb_scr[...]) + (cq - ck)
        s = jnp.where(col <= qpos, s, NEG)
        m = jnp.max(s, axis=1, keepdims=True)
        p = jnp.exp2(s - m)
        l = jnp.sum(p, axis=1, keepdims=True)
        o = _dot(p.astype(BF16), vb_scr[...]) / l
        y_ref[:, sl] = (o * g_ref[:, sl].astype(F32)).astype(BF16)


def _sb_sample_kernel(q_ref, kn_ref, vn_ref, g_ref, kp_ref, vp_ref, tri_ref, y_ref,
                      kb_scr, vb_scr, l_scr, u_scr, *, past_len, n_new):
    kpad = past_len + SAMPLE_KPAD
    row = lax.broadcasted_iota(jnp.int32, (n_new, kpad), 0)
    col = lax.broadcasted_iota(jnp.int32, (n_new, kpad), 1)
    mask = col < row + past_len
    k_past, v_past = _heads_on_lanes(kp_ref), _heads_on_lanes(vp_ref)
    for hh in range(HEAD_GROUP):
        sl = slice(hh * HEAD_DIM, (hh + 1) * HEAD_DIM)
        rows = slice(hh * n_new, (hh + 1) * n_new)
        _gather_keys(kb_scr, k_past, kn_ref, hh, past_len, n_new)
        _gather_keys(vb_scr.at[hh], v_past, vn_ref, hh, past_len, n_new)
        lsig, u = _log_gates(_dot_nt(q_ref[:, sl], kb_scr[...]))
        l_scr[rows, :] = lsig
        u_scr[rows, :] = jnp.where(mask, u, 0.0).astype(BF16)
    after = _dot(u_scr[...], tri_ref[...])
    for hh in range(HEAD_GROUP):
        sl = slice(hh * HEAD_DIM, (hh + 1) * HEAD_DIM)
        rows = slice(hh * n_new, (hh + 1) * n_new)
        a = jnp.where(mask, jnp.exp2(l_scr[rows, :] + after[rows, :]), 0.0)
        o = _dot(a.astype(BF16), vb_scr[hh])
        y_ref[:, sl] = (o * g_ref[:, sl].astype(F32)).astype(BF16)


def _sample_attention(kind, q, k_new, v_new, g, k_past, v_past, tri, lf_all_t=None):
    batch, past_len = k_past.shape[0], k_past.shape[1]
    n_new = q.shape[0] // batch
    kpad = past_len + SAMPLE_KPAD
    gw = HEAD_GROUP * HEAD_DIM
    tok_spec = pl.BlockSpec((n_new, gw), lambda b, hg: (b, hg))
    past_spec = pl.BlockSpec((None, past_len, HEAD_GROUP, HEAD_DIM), lambda b, hg: (b, 0, hg, 0))
    tri_spec = pl.BlockSpec((kpad, kpad), lambda b, hg: (0, 0))
    kp, vp = k_past, v_past
    if kind == "fox":
        body = partial(_fox_sample_kernel, past_len=past_len, n_new=n_new)
        in_specs = [tok_spec, tok_spec, tok_spec, tok_spec, past_spec, past_spec,
                    pl.BlockSpec((None, N_HEADS, kpad), lambda b, hg: (b, 0, 0)), tri_spec]
        args = (q, k_new, v_new, g, kp, vp, lf_all_t, tri)
        scratch = [pltpu.VMEM((kpad, HEAD_DIM), BF16), pltpu.VMEM((kpad, HEAD_DIM), BF16),
                   pltpu.VMEM((N_HEADS, kpad), F32)]
    else:
        body = partial(_sb_sample_kernel, past_len=past_len, n_new=n_new)
        in_specs = [tok_spec, tok_spec, tok_spec, tok_spec, past_spec, past_spec, tri_spec]
        args = (q, k_new, v_new, g, kp, vp, tri)
        scratch = [pltpu.VMEM((kpad, HEAD_DIM), BF16),
                   pltpu.VMEM((HEAD_GROUP, kpad, HEAD_DIM), BF16),
                   pltpu.VMEM((HEAD_GROUP * n_new, kpad), F32),
                   pltpu.VMEM((HEAD_GROUP * n_new, kpad), BF16)]
    return pl.pallas_call(
        body, grid=(batch, N_HEADS // HEAD_GROUP), in_specs=in_specs, out_specs=tok_spec,
        out_shape=jax.ShapeDtypeStruct((batch * n_new, D_INNER), BF16),
        scratch_shapes=scratch, compiler_params=_params(2), name=kind + "_sample_attention",
    )(*args)


def _tri(n, strict, lower):
    r = lax.broadcasted_iota(jnp.int32, (n, n), 0)
    c = lax.broadcasted_iota(jnp.int32, (n, n), 1)
    if lower:
        keep = (r > c) if strict else (r >= c)
    else:
        keep = (r < c) if strict else (r <= c)
    return keep.astype(BF16)


def kernel(x_prompt, x_sample, cache_fox_k, cache_fox_v, cache_fox_logf, cache_sb_k, cache_sb_v,
           norm_0, w_in_0, b_f_0, w_out_0, norm_1, w_in_1, w_out_1, norm_f):
    batch, seq, _ = x_prompt.shape
    dec_batch, dec_seq, _ = x_sample.shape
    past_len = cache_fox_k.shape[1]
    n_p, n_s = batch * seq, dec_batch * dec_seq
    kpad = past_len + SAMPLE_KPAD

    bf = jnp.pad(b_f_0, (0, LANES - N_HEADS)).reshape(1, LANES)
    xp = x_prompt.reshape(n_p, D_MODEL)
    xs = x_sample.reshape(n_s, D_MODEL)

    xnp, xns = _norm(xp, xs, norm_0)
    w0t = w_in_0.T
    qp, lfp, qs, lfs = _proj(xnp, xns, w0t, 0, "q", bf, transposed=True)
    kp0, kbp, ks0, kbs = _proj(xnp, xns, w0t, 1, "kv", transposed=True)
    vp0, vbp, vs0, vbs = _proj(xnp, xns, w0t, 2, "kv", transposed=True)
    gp, gs = _proj(xnp, xns, w0t, 3, "g", transposed=True)
    lfp = lfp.reshape(batch, seq, LANES)
    cum = _cumsum(lfp, _tri(CUM_T, strict=False, lower=True))
    yp = _fox_attention(qp, kbp, vbp, gp, cum, batch=batch, seq=seq)
    lfs = lfs[:, :N_HEADS].reshape(dec_batch, dec_seq, N_HEADS)
    lf_all = jnp.concatenate([cache_fox_logf, lfs], axis=1)
    lf_all_t = jnp.pad(jnp.swapaxes(lf_all, 1, 2), ((0, 0), (0, 0), (0, kpad - past_len - dec_seq)))
    ys = _sample_attention("fox", qs, kbs, vbs, gs, cache_fox_k, cache_fox_v,
                           _tri(kpad, strict=False, lower=False), lf_all_t)
    xp1, xnp, xs1, xns = _out_proj(xp, xs, yp, ys, w_out_0, norm_1, final=False)

    qp, qs = _proj(xnp, xns, w_in_1, 0, "q")
    kp1, kbp, ks1, kbs = _proj(xnp, xns, w_in_1, 1, "kv")
    vp1, vbp, vs1, vbs = _proj(xnp, xns, w_in_1, 2, "kv")
    gp, gs = _proj(xnp, xns, w_in_1, 3, "g")
    yp = _sb_attention(qp, kbp, vbp, gp, _tri(SB_SEG, strict=True, lower=True), batch=batch, seq=seq)
    ys = _sample_attention("sb", qs, kbs, vbs, gs, cache_sb_k, cache_sb_v,
                           _tri(kpad, strict=True, lower=True))
    y_prompt, y_sample = _out_proj(xp1, xs1, yp, ys, w_out_1, norm_f, final=True)

    hp = (batch, seq, N_HEADS, HEAD_DIM)
    hs = (dec_batch, dec_seq, N_HEADS, HEAD_DIM)
    return (y_prompt.reshape(batch, seq, D_MODEL), y_sample.reshape(dec_batch, dec_seq, D_MODEL),
            kp0.reshape(hp), vp0.reshape(hp), lfp[:, :, :N_HEADS],
            kp1.reshape(hp), vp1.reshape(hp),
            ks0.reshape(hs), vs0.reshape(hs), lfs,
            ks1.reshape(hs), vs1.reshape(hs))
```

```python
from functools import partial

import jax
import jax.numpy as jnp
from jax import lax
from jax.experimental import pallas as pl
from jax.experimental.pallas import tpu as pltpu

D_MODEL = 2048
N_HEADS = 16
HEAD_DIM = 128
D_INNER = N_HEADS * HEAD_DIM
EPS = 1e-6
SCALE = HEAD_DIM ** -0.5
LOG2E = 1.4426950408889634
NEG = -1e30

LANES = 128
F32 = jnp.float32
BF16 = jnp.bfloat16

VMEM_LIMIT = 56 * 1024 * 1024

ROW_TILE = 256
CAST_ROWS = 256
ATT_TK = 512
ATT_TQ = 2 * ATT_TK
ATT_RC = 32
ATT_RB = 256
SB_SEG = 256
CUM_T = 512
HEAD_GROUP = 8
SAMPLE_KPAD = 128


def _params(n_axes):
    return pltpu.CompilerParams(dimension_semantics=("arbitrary",) * n_axes,
                                vmem_limit_bytes=VMEM_LIMIT)


def _log_sigmoid(z):
    return jnp.minimum(z, 0.0) - jnp.log1p(jnp.exp(-jnp.abs(z)))


def _split3(x):
    x1 = x.astype(BF16)
    r1 = x - x1.astype(F32)
    x2 = r1.astype(BF16)
    x3 = (r1 - x2.astype(F32)).astype(BF16)
    return x1, x2, x3


def _dot(a, b):
    return jnp.dot(a, b, preferred_element_type=F32)


def _dot_nt(a, b):
    return lax.dot_general(a, b, (((1,), (1,)), ((), ())), preferred_element_type=F32)


def _tree(op, xs):
    while len(xs) > 1:
        xs = [op(xs[i], xs[i + 1]) for i in range(0, len(xs) - 1, 2)] + ([xs[-1]] if len(xs) % 2 else [])
    return xs[0]


def _live_slabs(r, diag_rows, n_cols, strict):
    starts = range(0, n_cols, LANES)
    if diag_rows is None or not diag_rows[0] <= r < diag_rows[1]:
        return [(c, None) for c in starts]
    first = r - diag_rows[0]
    shift = 1 if strict else 0
    live = []
    for c in starts:
        if c > first + ATT_RC - 1 - shift:
            break
        if c + LANES - 1 <= first - shift:
            live.append((c, None))
            continue
        row = lax.broadcasted_iota(jnp.int32, (ATT_RC, LANES), 0) + first
        col = lax.broadcasted_iota(jnp.int32, (ATT_RC, LANES), 1) + c
        live.append((c, (col < row) if strict else (col <= row)))
    return live


def _pad_lanes(parts, n_cols, dtype):
    dead = n_cols - len(parts) * LANES
    if dead:
        parts = parts + [jnp.zeros((ATT_RC, dead), dtype)]
    return jnp.concatenate(parts, axis=1)


def _row_blocks(rows):
    return [(r, min(r + ATT_RB, rows[1])) for r in range(rows[0], rows[1], ATT_RB)]


def _rmsnorm(x, w):
    ms = jnp.mean(x * x, axis=-1, keepdims=True)
    return (x * lax.rsqrt(ms + EPS)) * w


def _two_stream_specs(n_p_tiles, tm, n_s, tail):
    zeros = (0,) * len(tail)
    return (pl.BlockSpec((tm,) + tail, lambda m: (jnp.minimum(m, n_p_tiles - 1),) + zeros),
            pl.BlockSpec((n_s,) + tail, lambda m: (0,) + zeros))


def _resident(shape, index):
    return pl.BlockSpec(shape, lambda m: index, pipeline_mode=pl.Buffered(1))


def _cast_weight(w_ref, wb_scr, transposed=False):
    if transposed:
        for c in range(0, w_ref.shape[0], CAST_ROWS):
            wb_scr[:, c:c + CAST_ROWS] = w_ref[c:c + CAST_ROWS, :].T.astype(BF16)
        return

    def cast(i, carry):
        rows = pl.ds(pl.multiple_of(i * CAST_ROWS, CAST_ROWS), CAST_ROWS)
        wb_scr[rows, :] = w_ref[rows, :].astype(BF16)
        return carry
    lax.fori_loop(0, w_ref.shape[0] // CAST_ROWS, cast, 0)


def _proj_kernel(*refs, kind, with_f, transposed, n_p_tiles):
    xp_ref, xs_ref, w_ref = refs[:3]
    refs = refs[3:]
    if with_f:
        wf_ref, bf_ref, nw_ref = refs[:3]
        refs = refs[3:]
    n_out = {"q": 3 if with_f else 1, "kv": 2, "g": 1}[kind]
    outs_p, outs_s = refs[:n_out], refs[n_out:2 * n_out]
    scr = refs[2 * n_out:]
    wb_scr = scr[0]
    m = pl.program_id(0)

    @pl.when(m == 0)
    def _():
        _cast_weight(w_ref, wb_scr, transposed)
        if with_f:
            scr[1][...] = jnp.zeros(scr[1].shape, BF16)
            scr[1][0:N_HEADS, :] = wf_ref[...].astype(BF16)

    def run(x_ref, outs):
        x = x_ref[...]
        if with_f:
            x = _rmsnorm(x, nw_ref[...]).astype(BF16)
            outs[2][...] = x
        acc = _dot(x, wb_scr[...])
        if kind == "q":
            outs[0][...] = (acc * (SCALE * LOG2E)).astype(BF16)
            if with_f:
                outs[1][...] = _log_sigmoid(_dot_nt(x, scr[1][...]) + bf_ref[...])
        elif kind == "kv":
            outs[0][...] = acc.reshape(acc.shape[0], N_HEADS, HEAD_DIM)
            outs[1][...] = acc.astype(BF16)
        else:
            outs[0][...] = (acc * jax.nn.sigmoid(acc)).astype(BF16)

    @pl.when(m < n_p_tiles)
    def _():
        run(xp_ref, outs_p)

    @pl.when(m == n_p_tiles)
    def _():
        run(xs_ref, outs_s)


def _proj(xnp, xns, w, group, kind, bf=None, norm_w=None, *, transposed=False):
    n_p, n_s = xnp.shape[0], xns.shape[0]
    tm = ROW_TILE if kind == "kv" else 2 * ROW_TILE
    n_p_tiles = n_p // tm
    with_f = bf is not None
    assert transposed or not with_f
    flat = _two_stream_specs(n_p_tiles, tm, n_s, (D_INNER,))
    if transposed:
        w_spec = _resident((D_INNER, D_MODEL), (group, 0))
    else:
        w_spec = _resident((D_MODEL, D_INNER), (0, group))
    in_specs = [*flat, w_spec]
    args = [xnp, xns, w]
    scratch = [pltpu.VMEM((D_MODEL, D_INNER), BF16)]
    if with_f:
        in_specs += [_resident((N_HEADS, D_MODEL), (4 * D_INNER // N_HEADS, 0)),
                     _resident((1, LANES), (0, 0)), _resident((1, D_MODEL), (0, 0))]
        args += [w, bf, norm_w.reshape(1, D_MODEL)]
        scratch.append(pltpu.VMEM((LANES, D_MODEL), BF16))

    def shapes(tail, dtype):
        return [jax.ShapeDtypeStruct((n,) + tail, dtype) for n in (n_p, n_s)]

    if kind == "kv":
        heads = _two_stream_specs(n_p_tiles, tm, n_s, (N_HEADS, HEAD_DIM))
        out_specs = [heads[0], flat[0], heads[1], flat[1]]
        s4, s2 = shapes((N_HEADS, HEAD_DIM), F32), shapes((D_INNER,), BF16)
        out_shape = [s4[0], s2[0], s4[1], s2[1]]
    elif with_f:
        lanes = _two_stream_specs(n_p_tiles, tm, n_s, (LANES,))
        out_specs = [flat[0], lanes[0], flat[0], flat[1], lanes[1], flat[1]]
        s2, sl = shapes((D_INNER,), BF16), shapes((LANES,), F32)
        out_shape = [s2[0], sl[0], s2[0], s2[1], sl[1], s2[1]]
    else:
        out_specs = list(flat)
        out_shape = shapes((D_INNER,), BF16)
    return pl.pallas_call(
        partial(_proj_kernel, kind=kind, with_f=with_f, transposed=transposed,
                n_p_tiles=n_p_tiles),
        grid=(n_p_tiles + 1,), in_specs=in_specs, out_specs=out_specs, out_shape=out_shape,
        scratch_shapes=scratch, compiler_params=_params(1), name="proj_" + kind,
    )(*args)


def _out_proj_kernel(*refs, final, n_p_tiles):
    xp_ref, xs_ref, yp_ref, ys_ref, w_ref, nw_ref = refs[:6]
    n_out = 1 if final else 2
    outs_p, outs_s = refs[6:6 + n_out], refs[6 + n_out:6 + 2 * n_out]
    wb_scr = refs[6 + 2 * n_out]
    m = pl.program_id(0)

    @pl.when(m == 0)
    def _():
        _cast_weight(w_ref, wb_scr)

    def run(x_ref, y_ref, outs):
        r = x_ref[...] + _dot(y_ref[...], wb_scr[...])
        rn = _rmsnorm(r, nw_ref[...])
        if final:
            outs[0][...] = rn
        else:
            outs[0][...] = r
            outs[1][...] = rn.astype(BF16)

    @pl.when(m < n_p_tiles)
    def _():
        run(xp_ref, yp_ref, outs_p)

    @pl.when(m == n_p_tiles)
    def _():
        run(xs_ref, ys_ref, outs_s)


def _out_proj(xp, xs, yp, ys, w, norm_w, *, final):
    n_p, n_s = xp.shape[0], xs.shape[0]
    n_p_tiles = n_p // ROW_TILE
    flat = _two_stream_specs(n_p_tiles, ROW_TILE, n_s, (D_MODEL,))
    f32s = [jax.ShapeDtypeStruct((n, D_MODEL), F32) for n in (n_p, n_s)]
    bf16s = [jax.ShapeDtypeStruct((n, D_MODEL), BF16) for n in (n_p, n_s)]
    if final:
        out_specs, out_shape = list(flat), f32s
    else:
        out_specs = [flat[0], flat[0], flat[1], flat[1]]
        out_shape = [f32s[0], bf16s[0], f32s[1], bf16s[1]]
    return pl.pallas_call(
        partial(_out_proj_kernel, final=final, n_p_tiles=n_p_tiles), grid=(n_p_tiles + 1,),
        in_specs=[*flat, *flat, _resident((D_INNER, D_MODEL), (0, 0)),
                  pl.BlockSpec((1, D_MODEL), lambda m: (0, 0))],
        out_specs=out_specs, out_shape=out_shape,
        scratch_shapes=[pltpu.VMEM((D_INNER, D_MODEL), BF16)],
        compiler_params=_params(1), name="out_proj_final" if final else "out_proj",
    )(xp, xs, yp, ys, w, norm_w.reshape(1, D_MODEL))


def _cumsum_kernel(lf_ref, tri_ref, c_ref):
    n_chunks = lf_ref.shape[0] // CUM_T
    tri = tri_ref[...]
    carry = jnp.zeros((1, LANES), F32)
    for i in range(n_chunks):
        x1, x2, x3 = _split3(lf_ref[i * CUM_T:(i + 1) * CUM_T, :])
        c = (_dot(tri, x1) + _dot(tri, x2)) + _dot(tri, x3) + carry
        c_ref[i * CUM_T:(i + 1) * CUM_T, :] = c
        carry = c[CUM_T - 1:CUM_T, :]


def _cumsum(lf, tri_incl):
    b, t, _ = lf.shape
    return pl.pallas_call(
        _cumsum_kernel, grid=(b,),
        in_specs=[pl.BlockSpec((None, t, LANES), lambda i: (i, 0, 0)),
                  pl.BlockSpec((CUM_T, CUM_T), lambda i: (0, 0))],
        out_specs=pl.BlockSpec((None, t, LANES), lambda i: (i, 0, 0)),
        out_shape=jax.ShapeDtypeStruct((b, t, LANES), F32),
        compiler_params=_params(1), name="logf_cumsum",
    )(lf, tri_incl)


def _decay_columns(c_tile, h, for_keys):
    n = c_tile.shape[0]
    lane = lax.broadcasted_iota(jnp.int32, (n, LANES), 1)
    c = jnp.sum(jnp.where(lane == h, c_tile, 0.0), axis=1, keepdims=True) * LOG2E
    if for_keys:
        c = -c
    first_c, first_one = (3, 0) if for_keys else (0, 3)
    out = jnp.where((lane >= first_one) & (lane < first_one + 3), 1.0, 0.0)
    for i, part in enumerate(_split3(c)):
        out = jnp.where(lane == first_c + i, part.astype(F32), out)
    return out.astype(BF16)


def _fox_kernel(q_ref, k_ref, v_ref, g_ref, cq_ref, ck_ref, y_ref,
                kb_scr, qa_scr, s_scr, p_scr, m_scr, l_scr, acc_scr):
    h = pl.program_id(1)
    qi = pl.program_id(2)
    tq, tk = ATT_TQ, ATT_TK
    seq = k_ref.shape[0]
    all_rows, hi_rows = (0, tq), (tk, tq)

    @pl.when(qi == 0)
    def _():
        def fill(i, carry):
            rows = pl.ds(pl.multiple_of(i * tk, tk), tk)
            kb_scr[rows, 0:HEAD_DIM] = k_ref[rows, :]
            kb_scr[rows, HEAD_DIM:] = _decay_columns(ck_ref[rows, :], h, True)
            return carry
        lax.fori_loop(0, seq // tk, fill, 0)

    qa_scr[:, 0:HEAD_DIM] = q_ref[...]
    qa_scr[:, HEAD_DIM:] = _decay_columns(cq_ref[...], h, False)
    m_scr[...] = jnp.full((tq, LANES), NEG, F32)
    l_scr[...] = jnp.zeros((tq, LANES), F32)
    acc_scr[...] = jnp.zeros((tq, HEAD_DIM), F32)

    def scores(j, slot, rows):
        off = pl.multiple_of(j * tk, tk)
        s_scr[slot, rows[0]:rows[1], :] = _dot_nt(qa_scr[rows[0]:rows[1], :], kb_scr[pl.ds(off, tk), :])

    def softmax(slot, rows, diag_rows=None):
        for r in range(rows[0], rows[1], ATT_RC):
            sl = slice(r, r + ATT_RC)
            slabs = []
            for c, mask in _live_slabs(r, diag_rows, tk, strict=False):
                s = s_scr[slot, sl, c:c + LANES]
                slabs.append(s if mask is None else jnp.where(mask, s, NEG))
            m_prev = m_scr[sl, :]
            m_new = jnp.maximum(m_prev, jnp.max(_tree(jnp.maximum, slabs), axis=1, keepdims=True))
            alpha = jnp.exp2(m_prev - m_new)
            ps = [jnp.exp2(x - m_new) for x in slabs]
            l_scr[sl, :] = alpha * l_scr[sl, :] + _tree(jnp.add, ps)
            m_scr[sl, :] = m_new
            acc_scr[sl, :] = alpha * acc_scr[sl, :]
            p_scr[slot, sl, :] = _pad_lanes([p.astype(BF16) for p in ps], tk, BF16)

    def weighted_values(j, slot, rows):
        off = pl.multiple_of(j * tk, tk)
        sl = slice(rows[0], rows[1])
        acc_scr[sl, :] += _dot(p_scr[slot, sl, :], v_ref[pl.ds(off, tk), :])

    def attend(j, slot, rows, diag_rows=None):
        for blk in _row_blocks(rows):
            softmax(slot, blk, diag_rows)
            weighted_values(j, slot, blk)

    scores(0, 0, all_rows)

    def body(jj, carry):
        a = 2 * jj
        scores(a + 1, 1, all_rows)
        attend(a, 0, all_rows)
        scores(a + 2, 0, all_rows)
        attend(a + 1, 1, all_rows)
        return carry

    lax.fori_loop(0, qi, body, 0)

    d = 2 * qi
    scores(d + 1, 1, hi_rows)
    attend(d, 0, all_rows, diag_rows=(0, tk))
    attend(d + 1, 1, hi_rows, diag_rows=hi_rows)

    o = acc_scr[...] / jnp.sum(l_scr[...], axis=1, keepdims=True)
    y_ref[...] = (o * g_ref[...].astype(F32)).astype(BF16)


def _fox_attention(q, k, v, g, c, *, batch, seq):
    nq = seq // ATT_TQ
    tok_spec = pl.BlockSpec((ATT_TQ, HEAD_DIM), lambda b, h, i: (b * nq + i, h))
    seq_spec = pl.BlockSpec((seq, HEAD_DIM), lambda b, h, i: (b, h))
    return pl.pallas_call(
        _fox_kernel, grid=(batch, N_HEADS, nq),
        in_specs=[tok_spec, seq_spec, seq_spec, tok_spec,
                  pl.BlockSpec((None, ATT_TQ, LANES), lambda b, h, i: (b, i, 0)),
                  pl.BlockSpec((None, seq, LANES), lambda b, h, i: (b, 0, 0))],
        out_specs=tok_spec,
        out_shape=jax.ShapeDtypeStruct((batch * seq, D_INNER), BF16),
        scratch_shapes=[pltpu.VMEM((seq, 2 * HEAD_DIM), BF16),
                        pltpu.VMEM((ATT_TQ, 2 * HEAD_DIM), BF16),
                        pltpu.VMEM((2, ATT_TQ, ATT_TK), F32), pltpu.VMEM((2, ATT_TQ, ATT_TK), BF16),
                        pltpu.VMEM((ATT_TQ, LANES), F32), pltpu.VMEM((ATT_TQ, LANES), F32),
                        pltpu.VMEM((ATT_TQ, HEAD_DIM), F32)],
        compiler_params=_params(3), name="fox_attention",
    )(q, k, v, g, c, c)


def _log_gates(z):
    lsig = jnp.minimum(z, 0.0) - jnp.log2(1.0 + jnp.exp2(-jnp.abs(z)))
    return lsig, lsig - z


def _sb_kernel(q_ref, k_ref, v_ref, g_ref, tri_ref, y_ref,
               z_scr, l_scr, cs_scr, u_scr, a_scr, rs_scr, r_scr, acc_scr):
    qi = pl.program_id(2)
    tq, tk = ATT_TQ, ATT_TK
    seg = SB_SEG
    n_seg = tk // seg
    all_rows, lo_rows, hi_rows = (0, tq), (0, tk), (tk, tq)

    r_scr[...] = jnp.zeros((tq, LANES), F32)
    acc_scr[...] = jnp.zeros((tq, HEAD_DIM), F32)

    def scores(j, k, rows):
        off = pl.multiple_of(j * tk, tk)
        z_scr[k, rows[0]:rows[1], :] = _dot_nt(q_ref[rows[0]:rows[1], :], k_ref[pl.ds(off, tk), :])

    def log_gates(k, rows, diag_rows=None):
        for r in range(rows[0], rows[1], ATT_RC):
            sl = slice(r, r + ATT_RC)
            live = _live_slabs(r, diag_rows, tk, strict=True)
            width = len(live) * LANES
            lsig, u = _log_gates(z_scr[k, sl, 0:width])
            us = {c: u[:, c:c + LANES] if mask is None else jnp.where(mask, u[:, c:c + LANES], 0.0)
                  for c, mask in live}
            l_scr[k, sl, 0:width] = lsig
            u_scr[k, sl, :] = _pad_lanes([us[c].astype(BF16) for c, _ in live], tk, BF16)
            for sgi in range(n_seg):
                slabs = [us[c] for c, _ in live if sgi * seg <= c < (sgi + 1) * seg]
                if slabs:
                    rs_scr[k, sgi, sl, :] = jnp.broadcast_to(
                        jnp.sum(_tree(jnp.add, slabs), axis=1, keepdims=True), (ATT_RC, LANES))
                else:
                    rs_scr[k, sgi, sl, :] = jnp.zeros((ATT_RC, LANES), F32)

    def suffix_sums(k, rows):
        sl = slice(rows[0], rows[1])
        for sgi in range(n_seg):
            cols = slice(sgi * seg, (sgi + 1) * seg)
            cs_scr[k, sl, cols] = _dot(u_scr[k, sl, cols], tri_ref[...])

    def weights(k, rows, diag_rows=None):
        for r in range(rows[0], rows[1], ATT_RC):
            sl = slice(r, r + ATT_RC)
            base = r_scr[sl, :]
            live = _live_slabs(r, diag_rows, tk, strict=True)
            width = len(live) * LANES
            x = l_scr[k, sl, 0:width] + cs_scr[k, sl, 0:width]
            parts = {}
            for sgi in reversed(range(n_seg)):
                for c, mask in live:
                    if sgi * seg <= c < (sgi + 1) * seg:
                        a = jnp.exp2(x[:, c:c + LANES] + base)
                        parts[c] = a if mask is None else jnp.where(mask, a, 0.0)
                base = base + rs_scr[k, sgi, sl, :]
            a_scr[k, sl, :] = _pad_lanes([parts[c].astype(BF16) for c, _ in live], tk, BF16)
            r_scr[sl, :] = base

    def weighted_values(j, k, rows):
        off = pl.multiple_of(j * tk, tk)
        sl = slice(rows[0], rows[1])
        acc_scr[sl, :] += _dot(a_scr[k, sl, :], v_ref[pl.ds(off, tk), :])

    d = 2 * qi
    def gates(k, rows, diag_rows=None):
        for blk in _row_blocks(rows):
            log_gates(k, blk, diag_rows)
            suffix_sums(k, blk)

    def finish(j, k, rows, diag_rows=None):
        for blk in _row_blocks(rows):
            weights(k, blk, diag_rows)
            weighted_values(j, k, blk)

    scores(d + 1, 1, hi_rows)
    scores(d, 0, all_rows)
    gates(1, hi_rows, diag_rows=hi_rows)
    gates(0, all_rows, diag_rows=lo_rows)
    finish(d + 1, 1, hi_rows, diag_rows=hi_rows)
    finish(d, 0, all_rows, diag_rows=lo_rows)

    @pl.when(qi > 0)
    def _():
        scores(d - 1, 1, all_rows)
        scores(d - 2, 0, all_rows)
        gates(1, all_rows)
        gates(0, all_rows)

    def body(jj, carry):
        ja = d - 1 - 2 * jj
        scores(ja - 2, 1, all_rows)
        scores(ja - 3, 0, all_rows)
        for k, j in ((1, ja), (0, ja - 1)):
            for blk in _row_blocks(all_rows):
                finish(j, k, blk)
                gates(k, blk)
        return carry

    lax.fori_loop(0, qi - 1, body, 0)

    @pl.when(qi > 0)
    def _():
        finish(1, 1, all_rows)
        finish(0, 0, all_rows)

    y_ref[...] = (acc_scr[...] * g_ref[...].astype(F32)).astype(BF16)


def _sb_attention(q, k, v, g, tri_strict, *, batch, seq):
    nq = seq // ATT_TQ
    tok_spec = pl.BlockSpec((ATT_TQ, HEAD_DIM), lambda b, h, i: (b * nq + i, h))
    seq_spec = pl.BlockSpec((seq, HEAD_DIM), lambda b, h, i: (b, h))
    tile_f32 = pltpu.VMEM((2, ATT_TQ, ATT_TK), F32)
    tile_bf16 = pltpu.VMEM((2, ATT_TQ, ATT_TK), BF16)
    return pl.pallas_call(
        _sb_kernel, grid=(batch, N_HEADS, nq),
        in_specs=[tok_spec, seq_spec, seq_spec, tok_spec,
                  pl.BlockSpec((SB_SEG, SB_SEG), lambda b, h, i: (0, 0))],
        out_specs=tok_spec,
        out_shape=jax.ShapeDtypeStruct((batch * seq, D_INNER), BF16),
        scratch_shapes=[tile_f32, tile_f32, tile_f32, tile_bf16, tile_bf16,
                        pltpu.VMEM((2, ATT_TK // SB_SEG, ATT_TQ, LANES), F32),
                        pltpu.VMEM((ATT_TQ, LANES), F32), pltpu.VMEM((ATT_TQ, HEAD_DIM), F32)],
        compiler_params=_params(3), name="sb_attention",
    )(q, k, v, g, tri_strict)


def _heads_on_lanes(past_ref):
    x = past_ref[...]
    return x.reshape(x.shape[0], x.shape[1] * x.shape[2])


def _gather_keys(dst, past, new_ref, hh, past_len, n_new):
    sl = slice(hh * HEAD_DIM, (hh + 1) * HEAD_DIM)
    dst[0:past_len, :] = past[:, sl].astype(BF16)
    pad = jnp.zeros((SAMPLE_KPAD - n_new, HEAD_DIM), BF16)
    dst[past_len:past_len + SAMPLE_KPAD, :] = jnp.concatenate([new_ref[:, sl], pad], axis=0)


def _fox_sample_kernel(q_ref, kn_ref, vn_ref, g_ref, kp_ref, vp_ref, lft_ref, tri_ref, y_ref,
                       kb_scr, vb_scr, c_scr, *, past_len, n_new):
    hg = pl.program_id(1)
    kpad = past_len + SAMPLE_KPAD

    @pl.when(hg == 0)
    def _():
        tri = tri_ref[...]
        x1, x2, x3 = _split3(lft_ref[...])
        c_scr[...] = (_dot(x1, tri) + _dot(x2, tri)) + _dot(x3, tri)

    row = lax.broadcasted_iota(jnp.int32, (n_new, kpad), 0)
    col = lax.broadcasted_iota(jnp.int32, (n_new, kpad), 1)
    qpos = row + past_len
    k_past, v_past = _heads_on_lanes(kp_ref), _heads_on_lanes(vp_ref)
    for hh in range(HEAD_GROUP):
        sl = slice(hh * HEAD_DIM, (hh + 1) * HEAD_DIM)
        _gather_keys(kb_scr, k_past, kn_ref, hh, past_len, n_new)
        _gather_keys(vb_scr, v_past, vn_ref, hh, past_len, n_new)
        ck = c_scr[pl.ds(hg * HEAD_GROUP + hh, 1), :] * LOG2E
        cq = jnp.sum(jnp.where(col == qpos, ck, 0.0), axis=1, keepdims=True)
        s = _dot_nt(q_ref[:, sl], kb_scr[...]) + (cq - ck)
        s = jnp.where(col <= qpos, s, NEG)
        m = jnp.max(s, axis=1, keepdims=True)
        p = jnp.exp2(s - m)
        l = jnp.sum(p, axis=1, keepdims=True)
        o = _dot(p.astype(BF16), vb_scr[...]) / l
        y_ref[:, sl] = (o * g_ref[:, sl].astype(F32)).astype(BF16)


def _sb_sample_kernel(q_ref, kn_ref, vn_ref, g_ref, kp_ref, vp_ref, tri_ref, y_ref,
                      kb_scr, vb_scr, l_scr, u_scr, *, past_len, n_new):
    kpad = past_len + SAMPLE_KPAD
    row = lax.broadcasted_iota(jnp.int32, (n_new, kpad), 0)
    col = lax.broadcasted_iota(jnp.int32, (n_new, kpad), 1)
    mask = col < row + past_len
    k_past, v_past = _heads_on_lanes(kp_ref), _heads_on_lanes(vp_ref)
    for hh in range(HEAD_GROUP):
        sl = slice(hh * HEAD_DIM, (hh + 1) * HEAD_DIM)
        rows = slice(hh * n_new, (hh + 1) * n_new)
        _gather_keys(kb_scr, k_past, kn_ref, hh, past_len, n_new)
        _gather_keys(vb_scr.at[hh], v_past, vn_ref, hh, past_len, n_new)
        lsig, u = _log_gates(_dot_nt(q_ref[:, sl], kb_scr[...]))
        l_scr[rows, :] = lsig
        u_scr[rows, :] = jnp.where(mask, u, 0.0).astype(BF16)
    after = _dot(u_scr[...], tri_ref[...])
    for hh in range(HEAD_GROUP):
        sl = slice(hh * HEAD_DIM, (hh + 1) * HEAD_DIM)
        rows = slice(hh * n_new, (hh + 1) * n_new)
        a = jnp.where(mask, jnp.exp2(l_scr[rows, :] + after[rows, :]), 0.0)
        o = _dot(a.astype(BF16), vb_scr[hh])
        y_ref[:, sl] = (o * g_ref[:, sl].astype(F32)).astype(BF16)


def _sample_attention(kind, q, k_new, v_new, g, k_past, v_past, tri, lf_all_t=None):
    batch, past_len = k_past.shape[0], k_past.shape[1]
    n_new = q.shape[0] // batch
    kpad = past_len + SAMPLE_KPAD
    gw = HEAD_GROUP * HEAD_DIM
    tok_spec = pl.BlockSpec((n_new, gw), lambda b, hg: (b, hg))
    past_spec = pl.BlockSpec((None, past_len, HEAD_GROUP, HEAD_DIM), lambda b, hg: (b, 0, hg, 0))
    tri_spec = pl.BlockSpec((kpad, kpad), lambda b, hg: (0, 0))
    kp, vp = k_past, v_past
    if kind == "fox":
        body = partial(_fox_sample_kernel, past_len=past_len, n_new=n_new)
        in_specs = [tok_spec, tok_spec, tok_spec, tok_spec, past_spec, past_spec,
                    pl.BlockSpec((None, N_HEADS, kpad), lambda b, hg: (b, 0, 0)), tri_spec]
        args = (q, k_new, v_new, g, kp, vp, lf_all_t, tri)
        scratch = [pltpu.VMEM((kpad, HEAD_DIM), BF16), pltpu.VMEM((kpad, HEAD_DIM), BF16),
                   pltpu.VMEM((N_HEADS, kpad), F32)]
    else:
        body = partial(_sb_sample_kernel, past_len=past_len, n_new=n_new)
        in_specs = [tok_spec, tok_spec, tok_spec, tok_spec, past_spec, past_spec, tri_spec]
        args = (q, k_new, v_new, g, kp, vp, tri)
        scratch = [pltpu.VMEM((kpad, HEAD_DIM), BF16),
                   pltpu.VMEM((HEAD_GROUP, kpad, HEAD_DIM), BF16),
                   pltpu.VMEM((HEAD_GROUP * n_new, kpad), F32),
                   pltpu.VMEM((HEAD_GROUP * n_new, kpad), BF16)]
    return pl.pallas_call(
        body, grid=(batch, N_HEADS // HEAD_GROUP), in_specs=in_specs, out_specs=tok_spec,
        out_shape=jax.ShapeDtypeStruct((batch * n_new, D_INNER), BF16),
        scratch_shapes=scratch, compiler_params=_params(2), name=kind + "_sample_attention",
    )(*args)


def _tri(n, strict, lower):
    r = lax.broadcasted_iota(jnp.int32, (n, n), 0)
    c = lax.broadcasted_iota(jnp.int32, (n, n), 1)
    if lower:
        keep = (r > c) if strict else (r >= c)
    else:
        keep = (r < c) if strict else (r <= c)
    return keep.astype(BF16)


def kernel(x_prompt, x_sample, cache_fox_k, cache_fox_v, cache_fox_logf, cache_sb_k, cache_sb_v,
           norm_0, w_in_0, b_f_0, w_out_0, norm_1, w_in_1, w_out_1, norm_f):
    batch, seq, _ = x_prompt.shape
    dec_batch, dec_seq, _ = x_sample.shape
    past_len = cache_fox_k.shape[1]
    n_p, n_s = batch * seq, dec_batch * dec_seq
    kpad = past_len + SAMPLE_KPAD

    bf = jnp.pad(b_f_0, (0, LANES - N_HEADS)).reshape(1, LANES)
    xp = x_prompt.reshape(n_p, D_MODEL)
    xs = x_sample.reshape(n_s, D_MODEL)

    w0t = w_in_0.T
    qp, lfp, xnp, qs, lfs, xns = _proj(xp, xs, w0t, 0, "q", bf, norm_0, transposed=True)
    kp0, kbp, ks0, kbs = _proj(xnp, xns, w0t, 1, "kv", transposed=True)
    vp0, vbp, vs0, vbs = _proj(xnp, xns, w0t, 2, "kv", transposed=True)
    gp, gs = _proj(xnp, xns, w0t, 3, "g", transposed=True)
    lfp = lfp.reshape(batch, seq, LANES)
    cum = _cumsum(lfp, _tri(CUM_T, strict=False, lower=True))
    yp = _fox_attention(qp, kbp, vbp, gp, cum, batch=batch, seq=seq)
    lfs = lfs[:, :N_HEADS].reshape(dec_batch, dec_seq, N_HEADS)
    lf_all = jnp.concatenate([cache_fox_logf, lfs], axis=1)
    lf_all_t = jnp.pad(jnp.swapaxes(lf_all, 1, 2), ((0, 0), (0, 0), (0, kpad - past_len - dec_seq)))
    ys = _sample_attention("fox", qs, kbs, vbs, gs, cache_fox_k, cache_fox_v,
                           _tri(kpad, strict=False, lower=False), lf_all_t)
    xp1, xnp, xs1, xns = _out_proj(xp, xs, yp, ys, w_out_0, norm_1, final=False)

    qp, qs = _proj(xnp, xns, w_in_1, 0, "q")
    kp1, kbp, ks1, kbs = _proj(xnp, xns, w_in_1, 1, "kv")
    vp1, vbp, vs1, vbs = _proj(xnp, xns, w_in_1, 2, "kv")
    gp, gs = _proj(xnp, xns, w_in_1, 3, "g")
    yp = _sb_attention(qp, kbp, vbp, gp, _tri(SB_SEG, strict=True, lower=True), batch=batch, seq=seq)
    ys = _sample_attention("sb", qs, kbs, vbs, gs, cache_sb_k, cache_sb_v,
                           _tri(kpad, strict=True, lower=True))
    y_prompt, y_sample = _out_proj(xp1, xs1, yp, ys, w_out_1, norm_f, final=True)

    hp = (batch, seq, N_HEADS, HEAD_DIM)
    hs = (dec_batch, dec_seq, N_HEADS, HEAD_DIM)
    return (y_prompt.reshape(batch, seq, D_MODEL), y_sample.reshape(dec_batch, dec_seq, D_MODEL),
            kp0.reshape(hp), vp0.reshape(hp), lfp[:, :, :N_HEADS],
            kp1.reshape(hp), vp1.reshape(hp),
            ks0.reshape(hs), vs0.reshape(hs), lfs,
            ks1.reshape(hs), vs1.reshape(hs))
```

```python
from functools import partial

import jax
import jax.numpy as jnp
from jax import lax
from jax.experimental import pallas as pl
from jax.experimental.pallas import tpu as pltpu

D_MODEL = 2048
N_HEADS = 16
HEAD_DIM = 128
D_INNER = N_HEADS * HEAD_DIM
EPS = 1e-6
SCALE = HEAD_DIM ** -0.5
LOG2E = 1.4426950408889634
NEG = -1e30

LANES = 128
F32 = jnp.float32
BF16 = jnp.bfloat16

VMEM_LIMIT = 56 * 1024 * 1024

ROW_TILE = 256
CAST_ROWS = 256
ATT_TK = 512
ATT_TQ = 2 * ATT_TK
ATT_RC = 32
SB_SEG = 256
CUM_T = 512
HEAD_GROUP = 8
SAMPLE_KPAD = 128


def _params(n_axes):
    return pltpu.CompilerParams(dimension_semantics=("arbitrary",) * n_axes,
                                vmem_limit_bytes=VMEM_LIMIT)


def _log_sigmoid(z):
    return jnp.minimum(z, 0.0) - jnp.log1p(jnp.exp(-jnp.abs(z)))


def _split3(x):
    x1 = x.astype(BF16)
    r1 = x - x1.astype(F32)
    x2 = r1.astype(BF16)
    x3 = (r1 - x2.astype(F32)).astype(BF16)
    return x1, x2, x3


def _dot(a, b):
    return jnp.dot(a, b, preferred_element_type=F32)


def _dot_nt(a, b):
    return lax.dot_general(a, b, (((1,), (1,)), ((), ())), preferred_element_type=F32)


def _tree(op, xs):
    while len(xs) > 1:
        xs = [op(xs[i], xs[i + 1]) for i in range(0, len(xs) - 1, 2)] + ([xs[-1]] if len(xs) % 2 else [])
    return xs[0]


def _live_slabs(r, diag_rows, n_cols, strict):
    starts = range(0, n_cols, LANES)
    if diag_rows is None or not diag_rows[0] <= r < diag_rows[1]:
        return [(c, None) for c in starts]
    first = r - diag_rows[0]
    shift = 1 if strict else 0
    live = []
    for c in starts:
        if c > first + ATT_RC - 1 - shift:
            break
        if c + LANES - 1 <= first - shift:
            live.append((c, None))
            continue
        row = lax.broadcasted_iota(jnp.int32, (ATT_RC, LANES), 0) + first
        col = lax.broadcasted_iota(jnp.int32, (ATT_RC, LANES), 1) + c
        live.append((c, (col < row) if strict else (col <= row)))
    return live


def _pad_lanes(parts, n_cols, dtype):
    dead = n_cols - len(parts) * LANES
    if dead:
        parts = parts + [jnp.zeros((ATT_RC, dead), dtype)]
    return jnp.concatenate(parts, axis=1)


def _rmsnorm(x, w):
    ms = jnp.mean(x * x, axis=-1, keepdims=True)
    return (x * lax.rsqrt(ms + EPS)) * w


def _two_stream_specs(n_p_tiles, tm, n_s, tail):
    zeros = (0,) * len(tail)
    return (pl.BlockSpec((tm,) + tail, lambda m: (jnp.minimum(m, n_p_tiles - 1),) + zeros),
            pl.BlockSpec((n_s,) + tail, lambda m: (0,) + zeros))


def _resident(shape, index):
    return pl.BlockSpec(shape, lambda m: index, pipeline_mode=pl.Buffered(1))


def _cast_weight(w_ref, wb_scr, transposed=False):
    if transposed:
        for c in range(0, w_ref.shape[0], CAST_ROWS):
            wb_scr[:, c:c + CAST_ROWS] = w_ref[c:c + CAST_ROWS, :].T.astype(BF16)
        return

    def cast(i, carry):
        rows = pl.ds(pl.multiple_of(i * CAST_ROWS, CAST_ROWS), CAST_ROWS)
        wb_scr[rows, :] = w_ref[rows, :].astype(BF16)
        return carry
    lax.fori_loop(0, w_ref.shape[0] // CAST_ROWS, cast, 0)


def _proj_kernel(*refs, kind, with_f, transposed, n_p_tiles):
    xp_ref, xs_ref, w_ref = refs[:3]
    refs = refs[3:]
    if with_f:
        wf_ref, bf_ref, nw_ref = refs[:3]
        refs = refs[3:]
    n_out = {"q": 3 if with_f else 1, "kv": 2, "g": 1}[kind]
    outs_p, outs_s = refs[:n_out], refs[n_out:2 * n_out]
    scr = refs[2 * n_out:]
    wb_scr = scr[0]
    m = pl.program_id(0)

    @pl.when(m == 0)
    def _():
        _cast_weight(w_ref, wb_scr, transposed)
        if with_f:
            scr[1][...] = jnp.zeros(scr[1].shape, BF16)
            scr[1][0:N_HEADS, :] = wf_ref[...].astype(BF16)

    def run(x_ref, outs):
        x = x_ref[...]
        if with_f:
            x = _rmsnorm(x, nw_ref[...]).astype(BF16)
            outs[2][...] = x
        acc = _dot(x, wb_scr[...])
        if kind == "q":
            outs[0][...] = (acc * (SCALE * LOG2E)).astype(BF16)
            if with_f:
                outs[1][...] = _log_sigmoid(_dot_nt(x, scr[1][...]) + bf_ref[...])
        elif kind == "kv":
            outs[0][...] = acc.reshape(acc.shape[0], N_HEADS, HEAD_DIM)
            outs[1][...] = acc.astype(BF16)
        else:
            outs[0][...] = (acc * jax.nn.sigmoid(acc)).astype(BF16)

    @pl.when(m < n_p_tiles)
    def _():
        run(xp_ref, outs_p)

    @pl.when(m == n_p_tiles)
    def _():
        run(xs_ref, outs_s)


def _proj(xnp, xns, w, group, kind, bf=None, norm_w=None, *, transposed=False):
    n_p, n_s = xnp.shape[0], xns.shape[0]
    tm = ROW_TILE if kind == "kv" else 2 * ROW_TILE
    n_p_tiles = n_p // tm
    with_f = bf is not None
    assert transposed or not with_f
    flat = _two_stream_specs(n_p_tiles, tm, n_s, (D_INNER,))
    if transposed:
        w_spec = _resident((D_INNER, D_MODEL), (group, 0))
    else:
        w_spec = _resident((D_MODEL, D_INNER), (0, group))
    in_specs = [*flat, w_spec]
    args = [xnp, xns, w]
    scratch = [pltpu.VMEM((D_MODEL, D_INNER), BF16)]
    if with_f:
        in_specs += [_resident((N_HEADS, D_MODEL), (4 * D_INNER // N_HEADS, 0)),
                     _resident((1, LANES), (0, 0)), _resident((1, D_MODEL), (0, 0))]
        args += [w, bf, norm_w.reshape(1, D_MODEL)]
        scratch.append(pltpu.VMEM((LANES, D_MODEL), BF16))

    def shapes(tail, dtype):
        return [jax.ShapeDtypeStruct((n,) + tail, dtype) for n in (n_p, n_s)]

    if kind == "kv":
        heads = _two_stream_specs(n_p_tiles, tm, n_s, (N_HEADS, HEAD_DIM))
        out_specs = [heads[0], flat[0], heads[1], flat[1]]
        s4, s2 = shapes((N_HEADS, HEAD_DIM), F32), shapes((D_INNER,), BF16)
        out_shape = [s4[0], s2[0], s4[1], s2[1]]
    elif with_f:
        lanes = _two_stream_specs(n_p_tiles, tm, n_s, (LANES,))
        out_specs = [flat[0], lanes[0], flat[0], flat[1], lanes[1], flat[1]]
        s2, sl = shapes((D_INNER,), BF16), shapes((LANES,), F32)
        out_shape = [s2[0], sl[0], s2[0], s2[1], sl[1], s2[1]]
    else:
        out_specs = list(flat)
        out_shape = shapes((D_INNER,), BF16)
    return pl.pallas_call(
        partial(_proj_kernel, kind=kind, with_f=with_f, transposed=transposed,
                n_p_tiles=n_p_tiles),
        grid=(n_p_tiles + 1,), in_specs=in_specs, out_specs=out_specs, out_shape=out_shape,
        scratch_shapes=scratch, compiler_params=_params(1), name="proj_" + kind,
    )(*args)


def _out_proj_kernel(*refs, final, n_p_tiles):
    xp_ref, xs_ref, yp_ref, ys_ref, w_ref, nw_ref = refs[:6]
    n_out = 1 if final else 2
    outs_p, outs_s = refs[6:6 + n_out], refs[6 + n_out:6 + 2 * n_out]
    wb_scr = refs[6 + 2 * n_out]
    m = pl.program_id(0)

    @pl.when(m == 0)
    def _():
        _cast_weight(w_ref, wb_scr)

    def run(x_ref, y_ref, outs):
        r = x_ref[...] + _dot(y_ref[...], wb_scr[...])
        rn = _rmsnorm(r, nw_ref[...])
        if final:
            outs[0][...] = rn
        else:
            outs[0][...] = r
            outs[1][...] = rn.astype(BF16)

    @pl.when(m < n_p_tiles)
    def _():
        run(xp_ref, yp_ref, outs_p)

    @pl.when(m == n_p_tiles)
    def _():
        run(xs_ref, ys_ref, outs_s)


def _out_proj(xp, xs, yp, ys, w, norm_w, *, final):
    n_p, n_s = xp.shape[0], xs.shape[0]
    n_p_tiles = n_p // ROW_TILE
    flat = _two_stream_specs(n_p_tiles, ROW_TILE, n_s, (D_MODEL,))
    f32s = [jax.ShapeDtypeStruct((n, D_MODEL), F32) for n in (n_p, n_s)]
    bf16s = [jax.ShapeDtypeStruct((n, D_MODEL), BF16) for n in (n_p, n_s)]
    if final:
        out_specs, out_shape = list(flat), f32s
    else:
        out_specs = [flat[0], flat[0], flat[1], flat[1]]
        out_shape = [f32s[0], bf16s[0], f32s[1], bf16s[1]]
    return pl.pallas_call(
        partial(_out_proj_kernel, final=final, n_p_tiles=n_p_tiles), grid=(n_p_tiles + 1,),
        in_specs=[*flat, *flat, _resident((D_INNER, D_MODEL), (0, 0)),
                  pl.BlockSpec((1, D_MODEL), lambda m: (0, 0))],
        out_specs=out_specs, out_shape=out_shape,
        scratch_shapes=[pltpu.VMEM((D_INNER, D_MODEL), BF16)],
        compiler_params=_params(1), name="out_proj_final" if final else "out_proj",
    )(xp, xs, yp, ys, w, norm_w.reshape(1, D_MODEL))


def _cumsum_kernel(lf_ref, tri_ref, c_ref):
    n_chunks = lf_ref.shape[0] // CUM_T
    tri = tri_ref[...]
    carry = jnp.zeros((1, LANES), F32)
    for i in range(n_chunks):
        x1, x2, x3 = _split3(lf_ref[i * CUM_T:(i + 1) * CUM_T, :])
        c = (_dot(tri, x1) + _dot(tri, x2)) + _dot(tri, x3) + carry
        c_ref[i * CUM_T:(i + 1) * CUM_T, :] = c
        carry = c[CUM_T - 1:CUM_T, :]


def _cumsum(lf, tri_incl):
    b, t, _ = lf.shape
    return pl.pallas_call(
        _cumsum_kernel, grid=(b,),
        in_specs=[pl.BlockSpec((None, t, LANES), lambda i: (i, 0, 0)),
                  pl.BlockSpec((CUM_T, CUM_T), lambda i: (0, 0))],
        out_specs=pl.BlockSpec((None, t, LANES), lambda i: (i, 0, 0)),
        out_shape=jax.ShapeDtypeStruct((b, t, LANES), F32),
        compiler_params=_params(1), name="logf_cumsum",
    )(lf, tri_incl)


def _decay_columns(c_tile, h, for_keys):
    n = c_tile.shape[0]
    lane = lax.broadcasted_iota(jnp.int32, (n, LANES), 1)
    c = jnp.sum(jnp.where(lane == h, c_tile, 0.0), axis=1, keepdims=True) * LOG2E
    if for_keys:
        c = -c
    first_c, first_one = (3, 0) if for_keys else (0, 3)
    out = jnp.where((lane >= first_one) & (lane < first_one + 3), 1.0, 0.0)
    for i, part in enumerate(_split3(c)):
        out = jnp.where(lane == first_c + i, part.astype(F32), out)
    return out.astype(BF16)


def _fox_kernel(q_ref, k_ref, v_ref, g_ref, cq_ref, ck_ref, y_ref,
                kb_scr, qa_scr, s_scr, p_scr, m_scr, l_scr, acc_scr):
    h = pl.program_id(1)
    qi = pl.program_id(2)
    tq, tk = ATT_TQ, ATT_TK
    seq = k_ref.shape[0]
    all_rows, hi_rows = (0, tq), (tk, tq)

    @pl.when(qi == 0)
    def _():
        def fill(i, carry):
            rows = pl.ds(pl.multiple_of(i * tk, tk), tk)
            kb_scr[rows, 0:HEAD_DIM] = k_ref[rows, :]
            kb_scr[rows, HEAD_DIM:] = _decay_columns(ck_ref[rows, :], h, True)
            return carry
        lax.fori_loop(0, seq // tk, fill, 0)

    qa_scr[:, 0:HEAD_DIM] = q_ref[...]
    qa_scr[:, HEAD_DIM:] = _decay_columns(cq_ref[...], h, False)
    m_scr[...] = jnp.full((tq, LANES), NEG, F32)
    l_scr[...] = jnp.zeros((tq, LANES), F32)
    acc_scr[...] = jnp.zeros((tq, HEAD_DIM), F32)

    def scores(j, slot, rows):
        off = pl.multiple_of(j * tk, tk)
        s_scr[slot, rows[0]:rows[1], :] = _dot_nt(qa_scr[rows[0]:rows[1], :], kb_scr[pl.ds(off, tk), :])

    def softmax(slot, rows, diag_rows=None):
        for r in range(rows[0], rows[1], ATT_RC):
            sl = slice(r, r + ATT_RC)
            slabs = []
            for c, mask in _live_slabs(r, diag_rows, tk, strict=False):
                s = s_scr[slot, sl, c:c + LANES]
                slabs.append(s if mask is None else jnp.where(mask, s, NEG))
            m_prev = m_scr[sl, :]
            m_new = jnp.maximum(m_prev, jnp.max(_tree(jnp.maximum, slabs), axis=1, keepdims=True))
            alpha = jnp.exp2(m_prev - m_new)
            ps = [jnp.exp2(x - m_new) for x in slabs]
            l_scr[sl, :] = alpha * l_scr[sl, :] + _tree(jnp.add, ps)
            m_scr[sl, :] = m_new
            acc_scr[sl, :] = alpha * acc_scr[sl, :]
            p_scr[slot, sl, :] = _pad_lanes([p.astype(BF16) for p in ps], tk, BF16)

    def weighted_values(j, slot, rows):
        off = pl.multiple_of(j * tk, tk)
        sl = slice(rows[0], rows[1])
        acc_scr[sl, :] += _dot(p_scr[slot, sl, :], v_ref[pl.ds(off, tk), :])

    def attend(j, slot, rows, diag_rows=None):
        softmax(slot, rows, diag_rows)
        weighted_values(j, slot, rows)

    scores(0, 0, all_rows)

    def body(jj, carry):
        a = 2 * jj
        scores(a + 1, 1, all_rows)
        attend(a, 0, all_rows)
        scores(a + 2, 0, all_rows)
        attend(a + 1, 1, all_rows)
        return carry

    lax.fori_loop(0, qi, body, 0)

    d = 2 * qi
    scores(d + 1, 1, hi_rows)
    attend(d, 0, all_rows, diag_rows=(0, tk))
    attend(d + 1, 1, hi_rows, diag_rows=hi_rows)

    o = acc_scr[...] / jnp.sum(l_scr[...], axis=1, keepdims=True)
    y_ref[...] = (o * g_ref[...].astype(F32)).astype(BF16)


def _fox_attention(q, k, v, g, c, *, batch, seq):
    nq = seq // ATT_TQ
    tok_spec = pl.BlockSpec((ATT_TQ, HEAD_DIM), lambda b, h, i: (b * nq + i, h))
    seq_spec = pl.BlockSpec((seq, HEAD_DIM), lambda b, h, i: (b, h))
    return pl.pallas_call(
        _fox_kernel, grid=(batch, N_HEADS, nq),
        in_specs=[tok_spec, seq_spec, seq_spec, tok_spec,
                  pl.BlockSpec((None, ATT_TQ, LANES), lambda b, h, i: (b, i, 0)),
                  pl.BlockSpec((None, seq, LANES), lambda b, h, i: (b, 0, 0))],
        out_specs=tok_spec,
        out_shape=jax.ShapeDtypeStruct((batch * seq, D_INNER), BF16),
        scratch_shapes=[pltpu.VMEM((seq, 2 * HEAD_DIM), BF16),
                        pltpu.VMEM((ATT_TQ, 2 * HEAD_DIM), BF16),
                        pltpu.VMEM((2, ATT_TQ, ATT_TK), F32), pltpu.VMEM((2, ATT_TQ, ATT_TK), BF16),
                        pltpu.VMEM((ATT_TQ, LANES), F32), pltpu.VMEM((ATT_TQ, LANES), F32),
                        pltpu.VMEM((ATT_TQ, HEAD_DIM), F32)],
        compiler_params=_params(3), name="fox_attention",
    )(q, k, v, g, c, c)


def _log_gates(z):
    lsig = jnp.minimum(z, 0.0) - jnp.log2(1.0 + jnp.exp2(-jnp.abs(z)))
    return lsig, lsig - z


def _sb_kernel(q_ref, k_ref, v_ref, g_ref, tri_ref, y_ref,
               z_scr, l_scr, cs_scr, u_scr, a_scr, rs_scr, r_scr, acc_scr):
    qi = pl.program_id(2)
    tq, tk = ATT_TQ, ATT_TK
    seg = SB_SEG
    n_seg = tk // seg
    all_rows, lo_rows, hi_rows = (0, tq), (0, tk), (tk, tq)

    r_scr[...] = jnp.zeros((tq, LANES), F32)
    acc_scr[...] = jnp.zeros((tq, HEAD_DIM), F32)

    def scores(j, k, rows):
        off = pl.multiple_of(j * tk, tk)
        z_scr[k, rows[0]:rows[1], :] = _dot_nt(q_ref[rows[0]:rows[1], :], k_ref[pl.ds(off, tk), :])

    def log_gates(k, rows, diag_rows=None):
        for r in range(rows[0], rows[1], ATT_RC):
            sl = slice(r, r + ATT_RC)
            live = _live_slabs(r, diag_rows, tk, strict=True)
            width = len(live) * LANES
            lsig, u = _log_gates(z_scr[k, sl, 0:width])
            us = {c: u[:, c:c + LANES] if mask is None else jnp.where(mask, u[:, c:c + LANES], 0.0)
                  for c, mask in live}
            l_scr[k, sl, 0:width] = lsig
            u_scr[k, sl, :] = _pad_lanes([us[c].astype(BF16) for c, _ in live], tk, BF16)
            for sgi in range(n_seg):
                slabs = [us[c] for c, _ in live if sgi * seg <= c < (sgi + 1) * seg]
                if slabs:
                    rs_scr[k, sgi, sl, :] = jnp.broadcast_to(
                        jnp.sum(_tree(jnp.add, slabs), axis=1, keepdims=True), (ATT_RC, LANES))
                else:
                    rs_scr[k, sgi, sl, :] = jnp.zeros((ATT_RC, LANES), F32)

    def suffix_sums(k, rows):
        sl = slice(rows[0], rows[1])
        for sgi in range(n_seg):
            cols = slice(sgi * seg, (sgi + 1) * seg)
            cs_scr[k, sl, cols] = _dot(u_scr[k, sl, cols], tri_ref[...])

    def weights(k, rows, diag_rows=None):
        for r in range(rows[0], rows[1], ATT_RC):
            sl = slice(r, r + ATT_RC)
            base = r_scr[sl, :]
            live = _live_slabs(r, diag_rows, tk, strict=True)
            width = len(live) * LANES
            x = l_scr[k, sl, 0:width] + cs_scr[k, sl, 0:width]
            parts = {}
            for sgi in reversed(range(n_seg)):
                for c, mask in live:
                    if sgi * seg <= c < (sgi + 1) * seg:
                        a = jnp.exp2(x[:, c:c + LANES] + base)
                        parts[c] = a if mask is None else jnp.where(mask, a, 0.0)
                base = base + rs_scr[k, sgi, sl, :]
            a_scr[k, sl, :] = _pad_lanes([parts[c].astype(BF16) for c, _ in live], tk, BF16)
            r_scr[sl, :] = base

    def weighted_values(j, k, rows):
        off = pl.multiple_of(j * tk, tk)
        sl = slice(rows[0], rows[1])
        acc_scr[sl, :] += _dot(a_scr[k, sl, :], v_ref[pl.ds(off, tk), :])

    d = 2 * qi

    def gates(k, rows, diag_rows=None):
        log_gates(k, rows, diag_rows)
        suffix_sums(k, rows)

    def finish(j, k, rows, diag_rows=None):
        weights(k, rows, diag_rows)
        weighted_values(j, k, rows)

    scores(d + 1, 1, hi_rows)
    scores(d, 0, all_rows)
    gates(1, hi_rows, diag_rows=hi_rows)
    gates(0, all_rows, diag_rows=lo_rows)
    finish(d + 1, 1, hi_rows, diag_rows=hi_rows)
    finish(d, 0, all_rows, diag_rows=lo_rows)

    @pl.when(qi > 0)
    def _():
        scores(d - 1, 1, all_rows)
        gates(1, all_rows)
        scores(d - 2, 0, all_rows)
        gates(0, all_rows)

    def body(jj, carry):
        ja = d - 1 - 2 * jj
        finish(ja, 1, all_rows)
        finish(ja - 1, 0, all_rows)
        scores(ja - 2, 1, all_rows)
        gates(1, all_rows)
        scores(ja - 3, 0, all_rows)
        gates(0, all_rows)
        return carry

    lax.fori_loop(0, qi - 1, body, 0)

    @pl.when(qi > 0)
    def _():
        finish(1, 1, all_rows)
        finish(0, 0, all_rows)

    y_ref[...] = (acc_scr[...] * g_ref[...].astype(F32)).astype(BF16)


def _sb_attention(q, k, v, g, tri_strict, *, batch, seq):
    nq = seq // ATT_TQ
    tok_spec = pl.BlockSpec((ATT_TQ, HEAD_DIM), lambda b, h, i: (b * nq + i, h))
    seq_spec = pl.BlockSpec((seq, HEAD_DIM), lambda b, h, i: (b, h))
    tile_f32 = pltpu.VMEM((2, ATT_TQ, ATT_TK), F32)
    tile_bf16 = pltpu.VMEM((2, ATT_TQ, ATT_TK), BF16)
    return pl.pallas_call(
        _sb_kernel, grid=(batch, N_HEADS, nq),
        in_specs=[tok_spec, seq_spec, seq_spec, tok_spec,
                  pl.BlockSpec((SB_SEG, SB_SEG), lambda b, h, i: (0, 0))],
        out_specs=tok_spec,
        out_shape=jax.ShapeDtypeStruct((batch * seq, D_INNER), BF16),
        scratch_shapes=[tile_f32, tile_f32, tile_f32, tile_bf16, tile_bf16,
                        pltpu.VMEM((2, ATT_TK // SB_SEG, ATT_TQ, LANES), F32),
                        pltpu.VMEM((ATT_TQ, LANES), F32), pltpu.VMEM((ATT_TQ, HEAD_DIM), F32)],
        compiler_params=_params(3), name="sb_attention",
    )(q, k, v, g, tri_strict)


def _heads_on_lanes(past_ref):
    x = past_ref[...]
    return x.reshape(x.shape[0], x.shape[1] * x.shape[2])


def _gather_keys(dst, past, new_ref, hh, past_len, n_new):
    sl = slice(hh * HEAD_DIM, (hh + 1) * HEAD_DIM)
    dst[0:past_len, :] = past[:, sl].astype(BF16)
    pad = jnp.zeros((SAMPLE_KPAD - n_new, HEAD_DIM), BF16)
    dst[past_len:past_len + SAMPLE_KPAD, :] = jnp.concatenate([new_ref[:, sl], pad], axis=0)


def _fox_sample_kernel(q_ref, kn_ref, vn_ref, g_ref, kp_ref, vp_ref, lft_ref, tri_ref, y_ref,
                       kb_scr, vb_scr, c_scr, *, past_len, n_new):
    hg = pl.program_id(1)
    kpad = past_len + SAMPLE_KPAD

    @pl.when(hg == 0)
    def _():
        tri = tri_ref[...]
        x1, x2, x3 = _split3(lft_ref[...])
        c_scr[...] = (_dot(x1, tri) + _dot(x2, tri)) + _dot(x3, tri)

    row = lax.broadcasted_iota(jnp.int32, (n_new, kpad), 0)
    col = lax.broadcasted_iota(jnp.int32, (n_new, kpad), 1)
    qpos = row + past_len
    k_past, v_past = _heads_on_lanes(kp_ref), _heads_on_lanes(vp_ref)
    for hh in range(HEAD_GROUP):
        sl = slice(hh * HEAD_DIM, (hh + 1) * HEAD_DIM)
        _gather_keys(kb_scr, k_past, kn_ref, hh, past_len, n_new)
        _gather_keys(vb_scr, v_past, vn_ref, hh, past_len, n_new)
        ck = c_scr[pl.ds(hg * HEAD_GROUP + hh, 1), :] * LOG2E
        cq = jnp.sum(jnp.where(col == qpos, ck, 0.0), axis=1, keepdims=True)
        s = _dot_nt(q_ref[:, sl], kb_scr[...]) + (cq - ck)
        s = jnp.where(col <= qpos, s, NEG)
        m = jnp.max(s, axis=1, keepdims=True)
        p = jnp.exp2(s - m)
        l = jnp.sum(p, axis=1, keepdims=True)
        o = _dot(p.astype(BF16), vb_scr[...]) / l
        y_ref[:, sl] = (o * g_ref[:, sl].astype(F32)).astype(BF16)


def _sb_sample_kernel(q_ref, kn_ref, vn_ref, g_ref, kp_ref, vp_ref, tri_ref, y_ref,
                      kb_scr, vb_scr, l_scr, u_scr, *, past_len, n_new):
    kpad = past_len + SAMPLE_KPAD
    row = lax.broadcasted_iota(jnp.int32, (n_new, kpad), 0)
    col = lax.broadcasted_iota(jnp.int32, (n_new, kpad), 1)
    mask = col < row + past_len
    k_past, v_past = _heads_on_lanes(kp_ref), _heads_on_lanes(vp_ref)
    for hh in range(HEAD_GROUP):
        sl = slice(hh * HEAD_DIM, (hh + 1) * HEAD_DIM)
        rows = slice(hh * n_new, (hh + 1) * n_new)
        _gather_keys(kb_scr, k_past, kn_ref, hh, past_len, n_new)
        _gather_keys(vb_scr.at[hh], v_past, vn_ref, hh, past_len, n_new)
        lsig, u = _log_gates(_dot_nt(q_ref[:, sl], kb_scr[...]))
        l_scr[rows, :] = lsig
        u_scr[rows, :] = jnp.where(mask, u, 0.0).astype(BF16)
    after = _dot(u_scr[...], tri_ref[...])
    for hh in range(HEAD_GROUP):
        sl = slice(hh * HEAD_DIM, (hh + 1) * HEAD_DIM)
        rows = slice(hh * n_new, (hh + 1) * n_new)
        a = jnp.where(mask, jnp.exp2(l_scr[rows, :] + after[rows, :]), 0.0)
        o = _dot(a.astype(BF16), vb_scr[hh])
        y_ref[:, sl] = (o * g_ref[:, sl].astype(F32)).astype(BF16)


def _sample_attention(kind, q, k_new, v_new, g, k_past, v_past, tri, lf_all_t=None):
    batch, past_len = k_past.shape[0], k_past.shape[1]
    n_new = q.shape[0] // batch
    kpad = past_len + SAMPLE_KPAD
    gw = HEAD_GROUP * HEAD_DIM
    tok_spec = pl.BlockSpec((n_new, gw), lambda b, hg: (b, hg))
    past_spec = pl.BlockSpec((None, past_len, HEAD_GROUP, HEAD_DIM), lambda b, hg: (b, 0, hg, 0))
    tri_spec = pl.BlockSpec((kpad, kpad), lambda b, hg: (0, 0))
    kp, vp = k_past, v_past
    if kind == "fox":
        body = partial(_fox_sample_kernel, past_len=past_len, n_new=n_new)
        in_specs = [tok_spec, tok_spec, tok_spec, tok_spec, past_spec, past_spec,
                    pl.BlockSpec((None, N_HEADS, kpad), lambda b, hg: (b, 0, 0)), tri_spec]
        args = (q, k_new, v_new, g, kp, vp, lf_all_t, tri)
        scratch = [pltpu.VMEM((kpad, HEAD_DIM), BF16), pltpu.VMEM((kpad, HEAD_DIM), BF16),
                   pltpu.VMEM((N_HEADS, kpad), F32)]
    else:
        body = partial(_sb_sample_kernel, past_len=past_len, n_new=n_new)
        in_specs = [tok_spec, tok_spec, tok_spec, tok_spec, past_spec, past_spec, tri_spec]
        args = (q, k_new, v_new, g, kp, vp, tri)
        scratch = [pltpu.VMEM((kpad, HEAD_DIM), BF16),
                   pltpu.VMEM((HEAD_GROUP, kpad, HEAD_DIM), BF16),
                   pltpu.VMEM((HEAD_GROUP * n_new, kpad), F32),
                   pltpu.VMEM((HEAD_GROUP * n_new, kpad), BF16)]
    return pl.pallas_call(
        body, grid=(batch, N_HEADS // HEAD_GROUP), in_specs=in_specs, out_specs=tok_spec,
        out_shape=jax.ShapeDtypeStruct((batch * n_new, D_INNER), BF16),
        scratch_shapes=scratch, compiler_params=_params(2), name=kind + "_sample_attention",
    )(*args)


def _tri(n, strict, lower):
    r = lax.broadcasted_iota(jnp.int32, (n, n), 0)
    c = lax.broadcasted_iota(jnp.int32, (n, n), 1)
    if lower:
        keep = (r > c) if strict else (r >= c)
    else:
        keep = (r < c) if strict else (r <= c)
    return keep.astype(BF16)


def kernel(x_prompt, x_sample, cache_fox_k, cache_fox_v, cache_fox_logf, cache_sb_k, cache_sb_v,
           norm_0, w_in_0, b_f_0, w_out_0, norm_1, w_in_1, w_out_1, norm_f):
    batch, seq, _ = x_prompt.shape
    dec_batch, dec_seq, _ = x_sample.shape
    past_len = cache_fox_k.shape[1]
    n_p, n_s = batch * seq, dec_batch * dec_seq
    kpad = past_len + SAMPLE_KPAD

    bf = jnp.pad(b_f_0, (0, LANES - N_HEADS)).reshape(1, LANES)
    xp = x_prompt.reshape(n_p, D_MODEL)
    xs = x_sample.reshape(n_s, D_MODEL)

    w0t = w_in_0.T
    qp, lfp, xnp, qs, lfs, xns = _proj(xp, xs, w0t, 0, "q", bf, norm_0, transposed=True)
    kp0, kbp, ks0, kbs = _proj(xnp, xns, w0t, 1, "kv", transposed=True)
    vp0, vbp, vs0, vbs = _proj(xnp, xns, w0t, 2, "kv", transposed=True)
    gp, gs = _proj(xnp, xns, w0t, 3, "g", transposed=True)
    lfp = lfp.reshape(batch, seq, LANES)
    cum = _cumsum(lfp, _tri(CUM_T, strict=False, lower=True))
    yp = _fox_attention(qp, kbp, vbp, gp, cum, batch=batch, seq=seq)
    lfs = lfs[:, :N_HEADS].reshape(dec_batch, dec_seq, N_HEADS)
    lf_all = jnp.concatenate([cache_fox_logf, lfs], axis=1)
    lf_all_t = jnp.pad(jnp.swapaxes(lf_all, 1, 2), ((0, 0), (0, 0), (0, kpad - past_len - dec_seq)))
    ys = _sample_attention("fox", qs, kbs, vbs, gs, cache_fox_k, cache_fox_v,
                           _tri(kpad, strict=False, lower=False), lf_all_t)
    xp1, xnp, xs1, xns = _out_proj(xp, xs, yp, ys, w_out_0, norm_1, final=False)

    qp, qs = _proj(xnp, xns, w_in_1, 0, "q")
    kp1, kbp, ks1, kbs = _proj(xnp, xns, w_in_1, 1, "kv")
    vp1, vbp, vs1, vbs = _proj(xnp, xns, w_in_1, 2, "kv")
    gp, gs = _proj(xnp, xns, w_in_1, 3, "g")
    yp = _sb_attention(qp, kbp, vbp, gp, _tri(SB_SEG, strict=True, lower=True), batch=batch, seq=seq)
    ys = _sample_attention("sb", qs, kbs, vbs, gs, cache_sb_k, cache_sb_v,
                           _tri(kpad, strict=True, lower=True))
    y_prompt, y_sample = _out_proj(xp1, xs1, yp, ys, w_out_1, norm_f, final=True)

    hp = (batch, seq, N_HEADS, HEAD_DIM)
    hs = (dec_batch, dec_seq, N_HEADS, HEAD_DIM)
    return (y_prompt.reshape(batch, seq, D_MODEL), y_sample.reshape(dec_batch, dec_seq, D_MODEL),
            kp0.reshape(hp), vp0.reshape(hp), lfp[:, :, :N_HEADS],
            kp1.reshape(hp), vp1.reshape(hp),
            ks0.reshape(hs), vs0.reshape(hs), lfs,
            ks1.reshape(hs), vs1.reshape(hs))
```

```python
from functools import partial

import jax
import jax.numpy as jnp
from jax import lax
from jax.experimental import pallas as pl
from jax.experimental.pallas import tpu as pltpu

D_MODEL = 2048
N_HEADS = 16
HEAD_DIM = 128
D_INNER = N_HEADS * HEAD_DIM
EPS = 1e-6
SCALE = HEAD_DIM ** -0.5
LOG2E = 1.4426950408889634
NEG = -1e30

LANES = 128
F32 = jnp.float32
BF16 = jnp.bfloat16

V7X_VMEM_BYTES = 64 * 1024 * 1024
VMEM_LIMIT = V7X_VMEM_BYTES - 8 * 1024 * 1024

ROW_TILE = 256
CAST_ROWS = 256
ATT_TK = 512
ATT_TQ = 2 * ATT_TK
ATT_RC = 32
SB_SEG = 256
CUM_T = 512
HEAD_GROUP = 8
SAMPLE_KPAD = 128


def _params(n_axes):
    return pltpu.CompilerParams(dimension_semantics=("arbitrary",) * n_axes,
                                vmem_limit_bytes=VMEM_LIMIT)


def _log_sigmoid(z):
    return jnp.minimum(z, 0.0) - jnp.log1p(jnp.exp(-jnp.abs(z)))


def _split3(x):
    x1 = x.astype(BF16)
    r1 = x - x1.astype(F32)
    x2 = r1.astype(BF16)
    x3 = (r1 - x2.astype(F32)).astype(BF16)
    return x1, x2, x3


def _dot(a, b):
    return jnp.dot(a, b, preferred_element_type=F32)


def _dot_nt(a, b):
    return lax.dot_general(a, b, (((1,), (1,)), ((), ())), preferred_element_type=F32)


def _tree(op, xs):
    while len(xs) > 1:
        xs = [op(xs[i], xs[i + 1]) for i in range(0, len(xs) - 1, 2)] + ([xs[-1]] if len(xs) % 2 else [])
    return xs[0]


def _live_slabs(r, diag_rows, n_cols, strict):
    starts = range(0, n_cols, LANES)
    if diag_rows is None or not diag_rows[0] <= r < diag_rows[1]:
        return [(c, None) for c in starts]
    first = r - diag_rows[0]
    shift = 1 if strict else 0
    live = []
    for c in starts:
        if c > first + ATT_RC - 1 - shift:
            break
        if c + LANES - 1 <= first - shift:
            live.append((c, None))
            continue
        row = lax.broadcasted_iota(jnp.int32, (ATT_RC, LANES), 0) + first
        col = lax.broadcasted_iota(jnp.int32, (ATT_RC, LANES), 1) + c
        live.append((c, (col < row) if strict else (col <= row)))
    return live


def _pad_lanes(parts, n_cols, dtype):
    dead = n_cols - len(parts) * LANES
    if dead:
        parts = parts + [jnp.zeros((ATT_RC, dead), dtype)]
    return jnp.concatenate(parts, axis=1)


def _rmsnorm(x, w):
    ms = jnp.mean(x * x, axis=-1, keepdims=True)
    return (x * lax.rsqrt(ms + EPS)) * w


def _two_stream_specs(n_p_tiles, tm, n_s, tail):
    zeros = (0,) * len(tail)
    return (pl.BlockSpec((tm,) + tail, lambda m: (jnp.minimum(m, n_p_tiles - 1),) + zeros),
            pl.BlockSpec((n_s,) + tail, lambda m: (0,) + zeros))


def _resident(shape, index):
    return pl.BlockSpec(shape, lambda m: index, pipeline_mode=pl.Buffered(1))


def _cast_weight(w_ref, wb_scr, transposed=False):
    if transposed:
        for c in range(0, w_ref.shape[0], CAST_ROWS):
            wb_scr[:, c:c + CAST_ROWS] = w_ref[c:c + CAST_ROWS, :].T.astype(BF16)
        return

    def cast(i, carry):
        rows = pl.ds(pl.multiple_of(i * CAST_ROWS, CAST_ROWS), CAST_ROWS)
        wb_scr[rows, :] = w_ref[rows, :].astype(BF16)
        return carry
    lax.fori_loop(0, w_ref.shape[0] // CAST_ROWS, cast, 0)


def _proj_kernel(*refs, kind, with_f, transposed, n_p_tiles):
    xp_ref, xs_ref, w_ref = refs[:3]
    refs = refs[3:]
    if with_f:
        wf_ref, bf_ref, nw_ref = refs[:3]
        refs = refs[3:]
    n_out = {"q": 3 if with_f else 1, "kv": 2, "g": 1}[kind]
    outs_p, outs_s = refs[:n_out], refs[n_out:2 * n_out]
    scr = refs[2 * n_out:]
    wb_scr = scr[0]
    m = pl.program_id(0)

    @pl.when(m == 0)
    def _():
        _cast_weight(w_ref, wb_scr, transposed)
        if with_f:
            scr[1][...] = jnp.zeros(scr[1].shape, BF16)
            scr[1][0:N_HEADS, :] = wf_ref[...].astype(BF16)

    def run(x_ref, outs):
        x = x_ref[...]
        if with_f:
            x = _rmsnorm(x, nw_ref[...]).astype(BF16)
            outs[2][...] = x
        acc = _dot(x, wb_scr[...])
        if kind == "q":
            outs[0][...] = (acc * (SCALE * LOG2E)).astype(BF16)
            if with_f:
                outs[1][...] = _log_sigmoid(_dot_nt(x, scr[1][...]) + bf_ref[...])
        elif kind == "kv":
            outs[0][...] = acc.reshape(acc.shape[0], N_HEADS, HEAD_DIM)
            outs[1][...] = acc.astype(BF16)
        else:
            outs[0][...] = (acc * jax.nn.sigmoid(acc)).astype(BF16)

    @pl.when(m < n_p_tiles)
    def _():
        run(xp_ref, outs_p)

    @pl.when(m == n_p_tiles)
    def _():
        run(xs_ref, outs_s)


def _proj(xnp, xns, w, group, kind, bf=None, norm_w=None, *, transposed=False):
    n_p, n_s = xnp.shape[0], xns.shape[0]
    tm = ROW_TILE if kind == "kv" else 2 * ROW_TILE
    n_p_tiles = n_p // tm
    with_f = bf is not None
    assert transposed or not with_f
    flat = _two_stream_specs(n_p_tiles, tm, n_s, (D_INNER,))
    if transposed:
        w_spec = _resident((D_INNER, D_MODEL), (group, 0))
    else:
        w_spec = _resident((D_MODEL, D_INNER), (0, group))
    in_specs = [*flat, w_spec]
    args = [xnp, xns, w]
    scratch = [pltpu.VMEM((D_MODEL, D_INNER), BF16)]
    if with_f:
        in_specs += [_resident((N_HEADS, D_MODEL), (4 * D_INNER // N_HEADS, 0)),
                     _resident((1, LANES), (0, 0)), _resident((1, D_MODEL), (0, 0))]
        args += [w, bf, norm_w.reshape(1, D_MODEL)]
        scratch.append(pltpu.VMEM((LANES, D_MODEL), BF16))

    def shapes(tail, dtype):
        return [jax.ShapeDtypeStruct((n,) + tail, dtype) for n in (n_p, n_s)]

    if kind == "kv":
        heads = _two_stream_specs(n_p_tiles, tm, n_s, (N_HEADS, HEAD_DIM))
        out_specs = [heads[0], flat[0], heads[1], flat[1]]
        s4, s2 = shapes((N_HEADS, HEAD_DIM), F32), shapes((D_INNER,), BF16)
        out_shape = [s4[0], s2[0], s4[1], s2[1]]
    elif with_f:
        lanes = _two_stream_specs(n_p_tiles, tm, n_s, (LANES,))
        out_specs = [flat[0], lanes[0], flat[0], flat[1], lanes[1], flat[1]]
        s2, sl = shapes((D_INNER,), BF16), shapes((LANES,), F32)
        out_shape = [s2[0], sl[0], s2[0], s2[1], sl[1], s2[1]]
    else:
        out_specs = list(flat)
        out_shape = shapes((D_INNER,), BF16)
    return pl.pallas_call(
        partial(_proj_kernel, kind=kind, with_f=with_f, transposed=transposed,
                n_p_tiles=n_p_tiles),
        grid=(n_p_tiles + 1,), in_specs=in_specs, out_specs=out_specs, out_shape=out_shape,
        scratch_shapes=scratch, compiler_params=_params(1), name="proj_" + kind,
    )(*args)


def _out_proj_kernel(*refs, final, n_p_tiles):
    xp_ref, xs_ref, yp_ref, ys_ref, w_ref, nw_ref = refs[:6]
    n_out = 1 if final else 2
    outs_p, outs_s = refs[6:6 + n_out], refs[6 + n_out:6 + 2 * n_out]
    wb_scr = refs[6 + 2 * n_out]
    m = pl.program_id(0)

    @pl.when(m == 0)
    def _():
        _cast_weight(w_ref, wb_scr)

    def run(x_ref, y_ref, outs):
        r = x_ref[...] + _dot(y_ref[...], wb_scr[...])
        rn = _rmsnorm(r, nw_ref[...])
        if final:
            outs[0][...] = rn
        else:
            outs[0][...] = r
            outs[1][...] = rn.astype(BF16)

    @pl.when(m < n_p_tiles)
    def _():
        run(xp_ref, yp_ref, outs_p)

    @pl.when(m == n_p_tiles)
    def _():
        run(xs_ref, ys_ref, outs_s)


def _out_proj(xp, xs, yp, ys, w, norm_w, *, final):
    n_p, n_s = xp.shape[0], xs.shape[0]
    n_p_tiles = n_p // ROW_TILE
    flat = _two_stream_specs(n_p_tiles, ROW_TILE, n_s, (D_MODEL,))
    f32s = [jax.ShapeDtypeStruct((n, D_MODEL), F32) for n in (n_p, n_s)]
    bf16s = [jax.ShapeDtypeStruct((n, D_MODEL), BF16) for n in (n_p, n_s)]
    if final:
        out_specs, out_shape = list(flat), f32s
    else:
        out_specs = [flat[0], flat[0], flat[1], flat[1]]
        out_shape = [f32s[0], bf16s[0], f32s[1], bf16s[1]]
    return pl.pallas_call(
        partial(_out_proj_kernel, final=final, n_p_tiles=n_p_tiles), grid=(n_p_tiles + 1,),
        in_specs=[*flat, *flat, _resident((D_INNER, D_MODEL), (0, 0)),
                  pl.BlockSpec((1, D_MODEL), lambda m: (0, 0))],
        out_specs=out_specs, out_shape=out_shape,
        scratch_shapes=[pltpu.VMEM((D_INNER, D_MODEL), BF16)],
        compiler_params=_params(1), name="out_proj_final" if final else "out_proj",
    )(xp, xs, yp, ys, w, norm_w.reshape(1, D_MODEL))


def _cumsum_kernel(lf_ref, tri_ref, c_ref):
    n_chunks = lf_ref.shape[0] // CUM_T
    tri = tri_ref[...]
    carry = jnp.zeros((1, LANES), F32)
    for i in range(n_chunks):
        x1, x2, x3 = _split3(lf_ref[i * CUM_T:(i + 1) * CUM_T, :])
        c = (_dot(tri, x1) + _dot(tri, x2)) + _dot(tri, x3) + carry
        c_ref[i * CUM_T:(i + 1) * CUM_T, :] = c
        carry = c[CUM_T - 1:CUM_T, :]


def _cumsum(lf, tri_incl):
    b, t, _ = lf.shape
    return pl.pallas_call(
        _cumsum_kernel, grid=(b,),
        in_specs=[pl.BlockSpec((None, t, LANES), lambda i: (i, 0, 0)),
                  pl.BlockSpec((CUM_T, CUM_T), lambda i: (0, 0))],
        out_specs=pl.BlockSpec((None, t, LANES), lambda i: (i, 0, 0)),
        out_shape=jax.ShapeDtypeStruct((b, t, LANES), F32),
        compiler_params=_params(1), name="logf_cumsum",
    )(lf, tri_incl)


def _decay_columns(c_tile, h, for_keys):
    n = c_tile.shape[0]
    lane = lax.broadcasted_iota(jnp.int32, (n, LANES), 1)
    c = jnp.sum(jnp.where(lane == h, c_tile, 0.0), axis=1, keepdims=True) * LOG2E
    if for_keys:
        c = -c
    first_c, first_one = (3, 0) if for_keys else (0, 3)
    out = jnp.where((lane >= first_one) & (lane < first_one + 3), 1.0, 0.0)
    for i, part in enumerate(_split3(c)):
        out = jnp.where(lane == first_c + i, part.astype(F32), out)
    return out.astype(BF16)


def _fox_kernel(q_ref, k_ref, v_ref, g_ref, cq_ref, ck_ref, y_ref,
                kb_scr, qa_scr, s_scr, p_scr, m_scr, l_scr, acc_scr):
    h = pl.program_id(1)
    qi = pl.program_id(2)
    tq, tk = ATT_TQ, ATT_TK
    seq = k_ref.shape[0]
    all_rows, hi_rows = (0, tq), (tk, tq)

    @pl.when(qi == 0)
    def _():
        def fill(i, carry):
            rows = pl.ds(pl.multiple_of(i * tk, tk), tk)
            kb_scr[rows, 0:HEAD_DIM] = k_ref[rows, :]
            kb_scr[rows, HEAD_DIM:] = _decay_columns(ck_ref[rows, :], h, True)
            return carry
        lax.fori_loop(0, seq // tk, fill, 0)

    qa_scr[:, 0:HEAD_DIM] = q_ref[...]
    qa_scr[:, HEAD_DIM:] = _decay_columns(cq_ref[...], h, False)
    m_scr[...] = jnp.full((tq, LANES), NEG, F32)
    l_scr[...] = jnp.zeros((tq, LANES), F32)
    acc_scr[...] = jnp.zeros((tq, HEAD_DIM), F32)

    def scores(j, slot, rows):
        off = pl.multiple_of(j * tk, tk)
        s_scr[slot, rows[0]:rows[1], :] = _dot_nt(qa_scr[rows[0]:rows[1], :], kb_scr[pl.ds(off, tk), :])

    def softmax(slot, rows, diag_rows=None):
        for r in range(rows[0], rows[1], ATT_RC):
            sl = slice(r, r + ATT_RC)
            slabs = []
            for c, mask in _live_slabs(r, diag_rows, tk, strict=False):
                s = s_scr[slot, sl, c:c + LANES]
                slabs.append(s if mask is None else jnp.where(mask, s, NEG))
            m_prev = m_scr[sl, :]
            m_new = jnp.maximum(m_prev, jnp.max(_tree(jnp.maximum, slabs), axis=1, keepdims=True))
            alpha = jnp.exp2(m_prev - m_new)
            ps = [jnp.exp2(x - m_new) for x in slabs]
            l_scr[sl, :] = alpha * l_scr[sl, :] + _tree(jnp.add, ps)
            m_scr[sl, :] = m_new
            acc_scr[sl, :] = alpha * acc_scr[sl, :]
            p_scr[slot, sl, :] = _pad_lanes([p.astype(BF16) for p in ps], tk, BF16)

    def weighted_values(j, slot, rows):
        off = pl.multiple_of(j * tk, tk)
        sl = slice(rows[0], rows[1])
        acc_scr[sl, :] += _dot(p_scr[slot, sl, :], v_ref[pl.ds(off, tk), :])

    def attend(j, slot, rows, diag_rows=None):
        softmax(slot, rows, diag_rows)
        weighted_values(j, slot, rows)

    scores(0, 0, all_rows)

    def body(jj, carry):
        a = 2 * jj
        attend(a, 0, all_rows)
        scores(a + 1, 1, all_rows)
        scores(a + 2, 0, all_rows)
        attend(a + 1, 1, all_rows)
        return carry

    lax.fori_loop(0, qi, body, 0)

    d = 2 * qi
    scores(d + 1, 1, hi_rows)
    attend(d, 0, all_rows, diag_rows=(0, tk))
    attend(d + 1, 1, hi_rows, diag_rows=hi_rows)

    o = acc_scr[...] / jnp.sum(l_scr[...], axis=1, keepdims=True)
    y_ref[...] = (o * g_ref[...].astype(F32)).astype(BF16)


def _fox_attention(q, k, v, g, c, *, batch, seq):
    nq = seq // ATT_TQ
    tok_spec = pl.BlockSpec((ATT_TQ, HEAD_DIM), lambda b, h, i: (b * nq + i, h))
    seq_spec = pl.BlockSpec((seq, HEAD_DIM), lambda b, h, i: (b, h))
    return pl.pallas_call(
        _fox_kernel, grid=(batch, N_HEADS, nq),
        in_specs=[tok_spec, seq_spec, seq_spec, tok_spec,
                  pl.BlockSpec((None, ATT_TQ, LANES), lambda b, h, i: (b, i, 0)),
                  pl.BlockSpec((None, seq, LANES), lambda b, h, i: (b, 0, 0))],
        out_specs=tok_spec,
        out_shape=jax.ShapeDtypeStruct((batch * seq, D_INNER), BF16),
        scratch_shapes=[pltpu.VMEM((seq, 2 * HEAD_DIM), BF16),
                        pltpu.VMEM((ATT_TQ, 2 * HEAD_DIM), BF16),
                        pltpu.VMEM((2, ATT_TQ, ATT_TK), F32), pltpu.VMEM((2, ATT_TQ, ATT_TK), BF16),
                        pltpu.VMEM((ATT_TQ, LANES), F32), pltpu.VMEM((ATT_TQ, LANES), F32),
                        pltpu.VMEM((ATT_TQ, HEAD_DIM), F32)],
        compiler_params=_params(3), name="fox_attention",
    )(q, k, v, g, c, c)


def _log_gates(z):
    lsig = jnp.minimum(z, 0.0) - jnp.log2(1.0 + jnp.exp2(-jnp.abs(z)))
    return lsig, lsig - z


def _sb_kernel(q_ref, k_ref, v_ref, g_ref, tri_ref, y_ref,
               z_scr, l_scr, cs_scr, u_scr, a_scr, rs_scr, r_scr, acc_scr):
    qi = pl.program_id(2)
    tq, tk = ATT_TQ, ATT_TK
    seg = SB_SEG
    n_seg = tk // seg
    all_rows, lo_rows, hi_rows = (0, tq), (0, tk), (tk, tq)

    r_scr[...] = jnp.zeros((tq, LANES), F32)
    acc_scr[...] = jnp.zeros((tq, HEAD_DIM), F32)

    def scores(j, k, rows):
        off = pl.multiple_of(j * tk, tk)
        z_scr[k, rows[0]:rows[1], :] = _dot_nt(q_ref[rows[0]:rows[1], :], k_ref[pl.ds(off, tk), :])

    def log_gates(k, rows, diag_rows=None):
        for r in range(rows[0], rows[1], ATT_RC):
            sl = slice(r, r + ATT_RC)
            live = _live_slabs(r, diag_rows, tk, strict=True)
            width = len(live) * LANES
            lsig, u = _log_gates(z_scr[k, sl, 0:width])
            us = {c: u[:, c:c + LANES] if mask is None else jnp.where(mask, u[:, c:c + LANES], 0.0)
                  for c, mask in live}
            l_scr[k, sl, 0:width] = lsig
            u_scr[k, sl, :] = _pad_lanes([us[c].astype(BF16) for c, _ in live], tk, BF16)
            for sgi in range(n_seg):
                slabs = [us[c] for c, _ in live if sgi * seg <= c < (sgi + 1) * seg]
                if slabs:
                    rs_scr[k, sgi, sl, :] = jnp.broadcast_to(
                        jnp.sum(_tree(jnp.add, slabs), axis=1, keepdims=True), (ATT_RC, LANES))
                else:
                    rs_scr[k, sgi, sl, :] = jnp.zeros((ATT_RC, LANES), F32)

    def suffix_sums(k, rows):
        sl = slice(rows[0], rows[1])
        for sgi in range(n_seg):
            cols = slice(sgi * seg, (sgi + 1) * seg)
            cs_scr[k, sl, cols] = _dot(u_scr[k, sl, cols], tri_ref[...])

    def weights(k, rows, diag_rows=None):
        for r in range(rows[0], rows[1], ATT_RC):
            sl = slice(r, r + ATT_RC)
            base = r_scr[sl, :]
            live = _live_slabs(r, diag_rows, tk, strict=True)
            width = len(live) * LANES
            x = l_scr[k, sl, 0:width] + cs_scr[k, sl, 0:width]
            parts = {}
            for sgi in reversed(range(n_seg)):
                for c, mask in live:
                    if sgi * seg <= c < (sgi + 1) * seg:
                        a = jnp.exp2(x[:, c:c + LANES] + base)
                        parts[c] = a if mask is None else jnp.where(mask, a, 0.0)
                base = base + rs_scr[k, sgi, sl, :]
            a_scr[k, sl, :] = _pad_lanes([parts[c].astype(BF16) for c, _ in live], tk, BF16)
            r_scr[sl, :] = base

    def weighted_values(j, k, rows):
        off = pl.multiple_of(j * tk, tk)
        sl = slice(rows[0], rows[1])
        acc_scr[sl, :] += _dot(a_scr[k, sl, :], v_ref[pl.ds(off, tk), :])

    d = 2 * qi

    def gates(k, rows, diag_rows=None):
        log_gates(k, rows, diag_rows)
        suffix_sums(k, rows)

    def finish(j, k, rows, diag_rows=None):
        weights(k, rows, diag_rows)
        weighted_values(j, k, rows)

    scores(d + 1, 1, hi_rows)
    scores(d, 0, all_rows)
    gates(1, hi_rows, diag_rows=hi_rows)
    gates(0, all_rows, diag_rows=lo_rows)

    def finish_and_prepare(finish_pair, ja_next):
        finish_pair()
        scores(ja_next, 1, all_rows)
        scores(ja_next - 1, 0, all_rows)
        gates(1, all_rows)
        gates(0, all_rows)

    def finish_diagonal():
        finish(d + 1, 1, hi_rows, diag_rows=hi_rows)
        finish(d, 0, all_rows, diag_rows=lo_rows)

    @pl.when(qi == 0)
    def _():
        finish_diagonal()

    @pl.when(qi > 0)
    def _():
        finish_and_prepare(finish_diagonal, d - 1)

    def body(jj, carry):
        ja = d - 1 - 2 * jj

        def finish_pair():
            finish(ja, 1, all_rows)
            finish(ja - 1, 0, all_rows)

        finish_and_prepare(finish_pair, ja - 2)
        return carry

    lax.fori_loop(0, qi - 1, body, 0)

    @pl.when(qi > 0)
    def _():
        finish(1, 1, all_rows)
        finish(0, 0, all_rows)

    y_ref[...] = (acc_scr[...] * g_ref[...].astype(F32)).astype(BF16)


def _sb_attention(q, k, v, g, tri_strict, *, batch, seq):
    nq = seq // ATT_TQ
    tok_spec = pl.BlockSpec((ATT_TQ, HEAD_DIM), lambda b, h, i: (b * nq + i, h))
    seq_spec = pl.BlockSpec((seq, HEAD_DIM), lambda b, h, i: (b, h))
    tile_f32 = pltpu.VMEM((2, ATT_TQ, ATT_TK), F32)
    tile_bf16 = pltpu.VMEM((2, ATT_TQ, ATT_TK), BF16)
    return pl.pallas_call(
        _sb_kernel, grid=(batch, N_HEADS, nq),
        in_specs=[tok_spec, seq_spec, seq_spec, tok_spec,
                  pl.BlockSpec((SB_SEG, SB_SEG), lambda b, h, i: (0, 0))],
        out_specs=tok_spec,
        out_shape=jax.ShapeDtypeStruct((batch * seq, D_INNER), BF16),
        scratch_shapes=[tile_f32, tile_f32, tile_f32, tile_bf16, tile_bf16,
                        pltpu.VMEM((2, ATT_TK // SB_SEG, ATT_TQ, LANES), F32),
                        pltpu.VMEM((ATT_TQ, LANES), F32), pltpu.VMEM((ATT_TQ, HEAD_DIM), F32)],
        compiler_params=_params(3), name="sb_attention",
    )(q, k, v, g, tri_strict)


def _heads_on_lanes(past_ref):
    x = past_ref[...]
    return x.reshape(x.shape[0], x.shape[1] * x.shape[2])


def _gather_keys(dst, past, new_ref, hh, past_len, n_new):
    sl = slice(hh * HEAD_DIM, (hh + 1) * HEAD_DIM)
    dst[0:past_len, :] = past[:, sl].astype(BF16)
    pad = jnp.zeros((SAMPLE_KPAD - n_new, HEAD_DIM), BF16)
    dst[past_len:past_len + SAMPLE_KPAD, :] = jnp.concatenate([new_ref[:, sl], pad], axis=0)


def _fox_sample_kernel(q_ref, kn_ref, vn_ref, g_ref, kp_ref, vp_ref, lft_ref, tri_ref, y_ref,
                       kb_scr, vb_scr, c_scr, *, past_len, n_new):
    hg = pl.program_id(1)
    kpad = past_len + SAMPLE_KPAD

    @pl.when(hg == 0)
    def _():
        tri = tri_ref[...]
        x1, x2, x3 = _split3(lft_ref[...])
        c_scr[...] = (_dot(x1, tri) + _dot(x2, tri)) + _dot(x3, tri)

    row = lax.broadcasted_iota(jnp.int32, (n_new, kpad), 0)
    col = lax.broadcasted_iota(jnp.int32, (n_new, kpad), 1)
    qpos = row + past_len
    k_past, v_past = _heads_on_lanes(kp_ref), _heads_on_lanes(vp_ref)
    for hh in range(HEAD_GROUP):
        sl = slice(hh * HEAD_DIM, (hh + 1) * HEAD_DIM)
        _gather_keys(kb_scr, k_past, kn_ref, hh, past_len, n_new)
        _gather_keys(vb_scr, v_past, vn_ref, hh, past_len, n_new)
        ck = c_scr[pl.ds(hg * HEAD_GROUP + hh, 1), :] * LOG2E
        cq = jnp.sum(jnp.where(col == qpos, ck, 0.0), axis=1, keepdims=True)
        s = _dot_nt(q_ref[:, sl], kb_scr[...]) + (cq - ck)
        s = jnp.where(col <= qpos, s, NEG)
        m = jnp.max(s, axis=1, keepdims=True)
        p = jnp.exp2(s - m)
        l = jnp.sum(p, axis=1, keepdims=True)
        o = _dot(p.astype(BF16), vb_scr[...]) / l
        y_ref[:, sl] = (o * g_ref[:, sl].astype(F32)).astype(BF16)


def _sb_sample_kernel(q_ref, kn_ref, vn_ref, g_ref, kp_ref, vp_ref, tri_ref, y_ref,
                      kb_scr, vb_scr, l_scr, u_scr, *, past_len, n_new):
    kpad = past_len + SAMPLE_KPAD
    row = lax.broadcasted_iota(jnp.int32, (n_new, kpad), 0)
    col = lax.broadcasted_iota(jnp.int32, (n_new, kpad), 1)
    mask = col < row + past_len
    k_past, v_past = _heads_on_lanes(kp_ref), _heads_on_lanes(vp_ref)
    for hh in range(HEAD_GROUP):
        sl = slice(hh * HEAD_DIM, (hh + 1) * HEAD_DIM)
        rows = slice(hh * n_new, (hh + 1) * n_new)
        _gather_keys(kb_scr, k_past, kn_ref, hh, past_len, n_new)
        _gather_keys(vb_scr.at[hh], v_past, vn_ref, hh, past_len, n_new)
        lsig, u = _log_gates(_dot_nt(q_ref[:, sl], kb_scr[...]))
        l_scr[rows, :] = lsig
        u_scr[rows, :] = jnp.where(mask, u, 0.0).astype(BF16)
    after = _dot(u_scr[...], tri_ref[...])
    for hh in range(HEAD_GROUP):
        sl = slice(hh * HEAD_DIM, (hh + 1) * HEAD_DIM)
        rows = slice(hh * n_new, (hh + 1) * n_new)
        a = jnp.where(mask, jnp.exp2(l_scr[rows, :] + after[rows, :]), 0.0)
        o = _dot(a.astype(BF16), vb_scr[hh])
        y_ref[:, sl] = (o * g_ref[:, sl].astype(F32)).astype(BF16)


def _sample_attention(kind, q, k_new, v_new, g, k_past, v_past, tri, lf_all_t=None):
    batch, past_len = k_past.shape[0], k_past.shape[1]
    n_new = q.shape[0] // batch
    kpad = past_len + SAMPLE_KPAD
    gw = HEAD_GROUP * HEAD_DIM
    tok_spec = pl.BlockSpec((n_new, gw), lambda b, hg: (b, hg))
    past_spec = pl.BlockSpec((None, past_len, HEAD_GROUP, HEAD_DIM), lambda b, hg: (b, 0, hg, 0))
    tri_spec = pl.BlockSpec((kpad, kpad), lambda b, hg: (0, 0))
    kp, vp = k_past, v_past
    if kind == "fox":
        body = partial(_fox_sample_kernel, past_len=past_len, n_new=n_new)
        in_specs = [tok_spec, tok_spec, tok_spec, tok_spec, past_spec, past_spec,
                    pl.BlockSpec((None, N_HEADS, kpad), lambda b, hg: (b, 0, 0)), tri_spec]
        args = (q, k_new, v_new, g, kp, vp, lf_all_t, tri)
        scratch = [pltpu.VMEM((kpad, HEAD_DIM), BF16), pltpu.VMEM((kpad, HEAD_DIM), BF16),
                   pltpu.VMEM((N_HEADS, kpad), F32)]
    else:
        body = partial(_sb_sample_kernel, past_len=past_len, n_new=n_new)
        in_specs = [tok_spec, tok_spec, tok_spec, tok_spec, past_spec, past_spec, tri_spec]
        args = (q, k_new, v_new, g, kp, vp, tri)
        scratch = [pltpu.VMEM((kpad, HEAD_DIM), BF16),
                   pltpu.VMEM((HEAD_GROUP, kpad, HEAD_DIM), BF16),
                   pltpu.VMEM((HEAD_GROUP * n_new, kpad), F32),
                   pltpu.VMEM((HEAD_GROUP * n_new, kpad), BF16)]
    return pl.pallas_call(
        body, grid=(batch, N_HEADS // HEAD_GROUP), in_specs=in_specs, out_specs=tok_spec,
        out_shape=jax.ShapeDtypeStruct((batch * n_new, D_INNER), BF16),
        scratch_shapes=scratch, compiler_params=_params(2), name=kind + "_sample_attention",
    )(*args)


def _tri(n, strict, lower):
    r = lax.broadcasted_iota(jnp.int32, (n, n), 0)
    c = lax.broadcasted_iota(jnp.int32, (n, n), 1)
    if lower:
        keep = (r > c) if strict else (r >= c)
    else:
        keep = (r < c) if strict else (r <= c)
    return keep.astype(BF16)


def kernel(x_prompt, x_sample, cache_fox_k, cache_fox_v, cache_fox_logf, cache_sb_k, cache_sb_v,
           norm_0, w_in_0, b_f_0, w_out_0, norm_1, w_in_1, w_out_1, norm_f):
    batch, seq, _ = x_prompt.shape
    dec_batch, dec_seq, _ = x_sample.shape
    past_len = cache_fox_k.shape[1]
    n_p, n_s = batch * seq, dec_batch * dec_seq
    kpad = past_len + SAMPLE_KPAD

    bf = jnp.pad(b_f_0, (0, LANES - N_HEADS)).reshape(1, LANES)
    xp = x_prompt.reshape(n_p, D_MODEL)
    xs = x_sample.reshape(n_s, D_MODEL)

    w0t = w_in_0.T
    qp, lfp, xnp, qs, lfs, xns = _proj(xp, xs, w0t, 0, "q", bf, norm_0, transposed=True)
    kp0, kbp, ks0, kbs = _proj(xnp, xns, w0t, 1, "kv", transposed=True)
    vp0, vbp, vs0, vbs = _proj(xnp, xns, w0t, 2, "kv", transposed=True)
    gp, gs = _proj(xnp, xns, w0t, 3, "g", transposed=True)
    lfp = lfp.reshape(batch, seq, LANES)
    cum = _cumsum(lfp, _tri(CUM_T, strict=False, lower=True))
    yp = _fox_attention(qp, kbp, vbp, gp, cum, batch=batch, seq=seq)
    lfs = lfs[:, :N_HEADS].reshape(dec_batch, dec_seq, N_HEADS)
    lf_all = jnp.concatenate([cache_fox_logf, lfs], axis=1)
    lf_all_t = jnp.pad(jnp.swapaxes(lf_all, 1, 2), ((0, 0), (0, 0), (0, kpad - past_len - dec_seq)))
    ys = _sample_attention("fox", qs, kbs, vbs, gs, cache_fox_k, cache_fox_v,
                           _tri(kpad, strict=False, lower=False), lf_all_t)
    xp1, xnp, xs1, xns = _out_proj(xp, xs, yp, ys, w_out_0, norm_1, final=False)

    qp, qs = _proj(xnp, xns, w_in_1, 0, "q")
    kp1, kbp, ks1, kbs = _proj(xnp, xns, w_in_1, 1, "kv")
    vp1, vbp, vs1, vbs = _proj(xnp, xns, w_in_1, 2, "kv")
    gp, gs = _proj(xnp, xns, w_in_1, 3, "g")
    yp = _sb_attention(qp, kbp, vbp, gp, _tri(SB_SEG, strict=True, lower=True), batch=batch, seq=seq)
    ys = _sample_attention("sb", qs, kbs, vbs, gs, cache_sb_k, cache_sb_v,
                           _tri(kpad, strict=True, lower=True))
    y_prompt, y_sample = _out_proj(xp1, xs1, yp, ys, w_out_1, norm_f, final=True)

    hp = (batch, seq, N_HEADS, HEAD_DIM)
    hs = (dec_batch, dec_seq, N_HEADS, HEAD_DIM)
    return (y_prompt.reshape(batch, seq, D_MODEL), y_sample.reshape(dec_batch, dec_seq, D_MODEL),
            kp0.reshape(hp), vp0.reshape(hp), lfp[:, :, :N_HEADS],
            kp1.reshape(hp), vp1.reshape(hp),
            ks0.reshape(hs), vs0.reshape(hs), lfs,
            ks1.reshape(hs), vs1.reshape(hs))
```

```python
from functools import partial

import jax
import jax.numpy as jnp
from jax import lax
from jax.experimental import pallas as pl
from jax.experimental.pallas import tpu as pltpu

D_MODEL = 2048
N_HEADS = 16
HEAD_DIM = 128
D_INNER = N_HEADS * HEAD_DIM
EPS = 1e-6
SCALE = HEAD_DIM ** -0.5
LOG2E = 1.4426950408889634
NEG = -1e30

LANES = 128
F32 = jnp.float32
BF16 = jnp.bfloat16

V7X_VMEM_BYTES = 64 * 1024 * 1024
VMEM_LIMIT = V7X_VMEM_BYTES - 8 * 1024 * 1024

ROW_TILE = 256
CAST_ROWS = 256
ATT_TK = 512
ATT_TQ = 2 * ATT_TK
ATT_RC = 32
SB_SEG = 256
CUM_T = 512
HEAD_GROUP = 8
SAMPLE_KPAD = 128


def _params(n_axes):
    return pltpu.CompilerParams(dimension_semantics=("arbitrary",) * n_axes,
                                vmem_limit_bytes=VMEM_LIMIT)


def _log_sigmoid(z):
    return jnp.minimum(z, 0.0) - jnp.log1p(jnp.exp(-jnp.abs(z)))


def _split3(x):
    x1 = x.astype(BF16)
    r1 = x - x1.astype(F32)
    x2 = r1.astype(BF16)
    x3 = (r1 - x2.astype(F32)).astype(BF16)
    return x1, x2, x3


def _dot(a, b):
    return jnp.dot(a, b, preferred_element_type=F32)


def _dot_nt(a, b):
    return lax.dot_general(a, b, (((1,), (1,)), ((), ())), preferred_element_type=F32)


def _tree(op, xs):
    while len(xs) > 1:
        xs = [op(xs[i], xs[i + 1]) for i in range(0, len(xs) - 1, 2)] + ([xs[-1]] if len(xs) % 2 else [])
    return xs[0]


def _live_slabs(r, diag_rows, n_cols, strict):
    starts = range(0, n_cols, LANES)
    if diag_rows is None or not diag_rows[0] <= r < diag_rows[1]:
        return [(c, None) for c in starts]
    first = r - diag_rows[0]
    shift = 1 if strict else 0
    live = []
    for c in starts:
        if c > first + ATT_RC - 1 - shift:
            break
        if c + LANES - 1 <= first - shift:
            live.append((c, None))
            continue
        row = lax.broadcasted_iota(jnp.int32, (ATT_RC, LANES), 0) + first
        col = lax.broadcasted_iota(jnp.int32, (ATT_RC, LANES), 1) + c
        live.append((c, (col < row) if strict else (col <= row)))
    return live


def _pad_lanes(parts, n_cols, dtype):
    dead = n_cols - len(parts) * LANES
    if dead:
        parts = parts + [jnp.zeros((ATT_RC, dead), dtype)]
    return jnp.concatenate(parts, axis=1)


def _rmsnorm(x, w):
    ms = jnp.mean(x * x, axis=-1, keepdims=True)
    return (x * lax.rsqrt(ms + EPS)) * w


def _two_stream_specs(n_p_tiles, tm, n_s, tail):
    zeros = (0,) * len(tail)
    return (pl.BlockSpec((tm,) + tail, lambda m: (jnp.minimum(m, n_p_tiles - 1),) + zeros),
            pl.BlockSpec((n_s,) + tail, lambda m: (0,) + zeros))


def _resident(shape, index):
    return pl.BlockSpec(shape, lambda m: index, pipeline_mode=pl.Buffered(1))


def _cast_weight(w_ref, wb_scr, transposed=False):
    if transposed:
        for c in range(0, w_ref.shape[0], CAST_ROWS):
            wb_scr[:, c:c + CAST_ROWS] = w_ref[c:c + CAST_ROWS, :].T.astype(BF16)
        return

    def cast(i, carry):
        rows = pl.ds(pl.multiple_of(i * CAST_ROWS, CAST_ROWS), CAST_ROWS)
        wb_scr[rows, :] = w_ref[rows, :].astype(BF16)
        return carry
    lax.fori_loop(0, w_ref.shape[0] // CAST_ROWS, cast, 0)


def _proj_kernel(*refs, kind, with_f, transposed, n_p_tiles):
    xp_ref, xs_ref, w_ref = refs[:3]
    refs = refs[3:]
    if with_f:
        wf_ref, bf_ref, nw_ref = refs[:3]
        refs = refs[3:]
    n_out = {"q": 3 if with_f else 1, "kv": 2, "g": 1}[kind]
    outs_p, outs_s = refs[:n_out], refs[n_out:2 * n_out]
    scr = refs[2 * n_out:]
    wb_scr = scr[0]
    m = pl.program_id(0)

    @pl.when(m == 0)
    def _():
        _cast_weight(w_ref, wb_scr, transposed)
        if with_f:
            scr[1][...] = jnp.zeros(scr[1].shape, BF16)
            scr[1][0:N_HEADS, :] = wf_ref[...].astype(BF16)

    def run(x_ref, outs):
        x = x_ref[...]
        if with_f:
            x = _rmsnorm(x, nw_ref[...]).astype(BF16)
            outs[2][...] = x
        acc = _dot(x, wb_scr[...])
        if kind == "q":
            outs[0][...] = (acc * (SCALE * LOG2E)).astype(BF16)
            if with_f:
                outs[1][...] = _log_sigmoid(_dot_nt(x, scr[1][...]) + bf_ref[...])
        elif kind == "kv":
            outs[0][...] = acc.reshape(acc.shape[0], N_HEADS, HEAD_DIM)
            outs[1][...] = acc.astype(BF16)
        else:
            outs[0][...] = (acc * jax.nn.sigmoid(acc)).astype(BF16)

    @pl.when(m < n_p_tiles)
    def _():
        run(xp_ref, outs_p)

    @pl.when(m == n_p_tiles)
    def _():
        run(xs_ref, outs_s)


def _proj(xnp, xns, w, group, kind, bf=None, norm_w=None, *, transposed=False):
    n_p, n_s = xnp.shape[0], xns.shape[0]
    tm = ROW_TILE if kind == "kv" else 2 * ROW_TILE
    n_p_tiles = n_p // tm
    with_f = bf is not None
    assert transposed or not with_f
    flat = _two_stream_specs(n_p_tiles, tm, n_s, (D_INNER,))
    if transposed:
        w_spec = _resident((D_INNER, D_MODEL), (group, 0))
    else:
        w_spec = _resident((D_MODEL, D_INNER), (0, group))
    in_specs = [*flat, w_spec]
    args = [xnp, xns, w]
    scratch = [pltpu.VMEM((D_MODEL, D_INNER), BF16)]
    if with_f:
        in_specs += [_resident((N_HEADS, D_MODEL), (4 * D_INNER // N_HEADS, 0)),
                     _resident((1, LANES), (0, 0)), _resident((1, D_MODEL), (0, 0))]
        args += [w, bf, norm_w.reshape(1, D_MODEL)]
        scratch.append(pltpu.VMEM((LANES, D_MODEL), BF16))

    def shapes(tail, dtype):
        return [jax.ShapeDtypeStruct((n,) + tail, dtype) for n in (n_p, n_s)]

    if kind == "kv":
        heads = _two_stream_specs(n_p_tiles, tm, n_s, (N_HEADS, HEAD_DIM))
        out_specs = [heads[0], flat[0], heads[1], flat[1]]
        s4, s2 = shapes((N_HEADS, HEAD_DIM), F32), shapes((D_INNER,), BF16)
        out_shape = [s4[0], s2[0], s4[1], s2[1]]
    elif with_f:
        lanes = _two_stream_specs(n_p_tiles, tm, n_s, (LANES,))
        out_specs = [flat[0], lanes[0], flat[0], flat[1], lanes[1], flat[1]]
        s2, sl = shapes((D_INNER,), BF16), shapes((LANES,), F32)
        out_shape = [s2[0], sl[0], s2[0], s2[1], sl[1], s2[1]]
    else:
        out_specs = list(flat)
        out_shape = shapes((D_INNER,), BF16)
    return pl.pallas_call(
        partial(_proj_kernel, kind=kind, with_f=with_f, transposed=transposed,
                n_p_tiles=n_p_tiles),
        grid=(n_p_tiles + 1,), in_specs=in_specs, out_specs=out_specs, out_shape=out_shape,
        scratch_shapes=scratch, compiler_params=_params(1), name="proj_" + kind,
    )(*args)


def _out_proj_kernel(*refs, final, n_p_tiles):
    xp_ref, xs_ref, yp_ref, ys_ref, w_ref, nw_ref = refs[:6]
    n_out = 1 if final else 2
    outs_p, outs_s = refs[6:6 + n_out], refs[6 + n_out:6 + 2 * n_out]
    wb_scr = refs[6 + 2 * n_out]
    m = pl.program_id(0)

    @pl.when(m == 0)
    def _():
        _cast_weight(w_ref, wb_scr)

    def run(x_ref, y_ref, outs):
        r = x_ref[...] + _dot(y_ref[...], wb_scr[...])
        rn = _rmsnorm(r, nw_ref[...])
        if final:
            outs[0][...] = rn
        else:
            outs[0][...] = r
            outs[1][...] = rn.astype(BF16)

    @pl.when(m < n_p_tiles)
    def _():
        run(xp_ref, yp_ref, outs_p)

    @pl.when(m == n_p_tiles)
    def _():
        run(xs_ref, ys_ref, outs_s)


def _out_proj(xp, xs, yp, ys, w, norm_w, *, final):
    n_p, n_s = xp.shape[0], xs.shape[0]
    n_p_tiles = n_p // ROW_TILE
    flat = _two_stream_specs(n_p_tiles, ROW_TILE, n_s, (D_MODEL,))
    f32s = [jax.ShapeDtypeStruct((n, D_MODEL), F32) for n in (n_p, n_s)]
    bf16s = [jax.ShapeDtypeStruct((n, D_MODEL), BF16) for n in (n_p, n_s)]
    if final:
        out_specs, out_shape = list(flat), f32s
    else:
        out_specs = [flat[0], flat[0], flat[1], flat[1]]
        out_shape = [f32s[0], bf16s[0], f32s[1], bf16s[1]]
    return pl.pallas_call(
        partial(_out_proj_kernel, final=final, n_p_tiles=n_p_tiles), grid=(n_p_tiles + 1,),
        in_specs=[*flat, *flat, _resident((D_INNER, D_MODEL), (0, 0)),
                  pl.BlockSpec((1, D_MODEL), lambda m: (0, 0))],
        out_specs=out_specs, out_shape=out_shape,
        scratch_shapes=[pltpu.VMEM((D_INNER, D_MODEL), BF16)],
        compiler_params=_params(1), name="out_proj_final" if final else "out_proj",
    )(xp, xs, yp, ys, w, norm_w.reshape(1, D_MODEL))


def _cumsum_kernel(lf_ref, tri_ref, c_ref):
    n_chunks = lf_ref.shape[0] // CUM_T
    tri = tri_ref[...]
    carry = jnp.zeros((1, LANES), F32)
    for i in range(n_chunks):
        x1, x2, x3 = _split3(lf_ref[i * CUM_T:(i + 1) * CUM_T, :])
        c = (_dot(tri, x1) + _dot(tri, x2)) + _dot(tri, x3) + carry
        c_ref[i * CUM_T:(i + 1) * CUM_T, :] = c
        carry = c[CUM_T - 1:CUM_T, :]


def _cumsum(lf, tri_incl):
    b, t, _ = lf.shape
    return pl.pallas_call(
        _cumsum_kernel, grid=(b,),
        in_specs=[pl.BlockSpec((None, t, LANES), lambda i: (i, 0, 0)),
                  pl.BlockSpec((CUM_T, CUM_T), lambda i: (0, 0))],
        out_specs=pl.BlockSpec((None, t, LANES), lambda i: (i, 0, 0)),
        out_shape=jax.ShapeDtypeStruct((b, t, LANES), F32),
        compiler_params=_params(1), name="logf_cumsum",
    )(lf, tri_incl)


def _decay_columns(c_tile, h, for_keys):
    n = c_tile.shape[0]
    lane = lax.broadcasted_iota(jnp.int32, (n, LANES), 1)
    c = jnp.sum(jnp.where(lane == h, c_tile, 0.0), axis=1, keepdims=True) * LOG2E
    if for_keys:
        c = -c
    first_c, first_one = (3, 0) if for_keys else (0, 3)
    out = jnp.where((lane >= first_one) & (lane < first_one + 3), 1.0, 0.0)
    for i, part in enumerate(_split3(c)):
        out = jnp.where(lane == first_c + i, part.astype(F32), out)
    return out.astype(BF16)


def _fox_kernel(q_ref, k_ref, v_ref, g_ref, cq_ref, ck_ref, y_ref,
                kb_scr, qa_scr, s_scr, p_scr, m_scr, l_scr, acc_scr):
    h = pl.program_id(1)
    qi = pl.program_id(2)
    tq, tk = ATT_TQ, ATT_TK
    seq = k_ref.shape[0]
    all_rows, hi_rows = (0, tq), (tk, tq)

    @pl.when(qi == 0)
    def _():
        def fill(i, carry):
            rows = pl.ds(pl.multiple_of(i * tk, tk), tk)
            kb_scr[rows, 0:HEAD_DIM] = k_ref[rows, :]
            kb_scr[rows, HEAD_DIM:] = _decay_columns(ck_ref[rows, :], h, True)
            return carry
        lax.fori_loop(0, seq // tk, fill, 0)

    qa_scr[:, 0:HEAD_DIM] = q_ref[...]
    qa_scr[:, HEAD_DIM:] = _decay_columns(cq_ref[...], h, False)
    m_scr[...] = jnp.full((tq, LANES), NEG, F32)
    l_scr[...] = jnp.zeros((tq, LANES), F32)
    acc_scr[...] = jnp.zeros((tq, HEAD_DIM), F32)

    def scores(j, slot, rows):
        off = pl.multiple_of(j * tk, tk)
        s_scr[slot, rows[0]:rows[1], :] = _dot_nt(qa_scr[rows[0]:rows[1], :], kb_scr[pl.ds(off, tk), :])

    def softmax(slot, rows, diag_rows=None):
        for r in range(rows[0], rows[1], ATT_RC):
            sl = slice(r, r + ATT_RC)
            slabs = []
            for c, mask in _live_slabs(r, diag_rows, tk, strict=False):
                s = s_scr[slot, sl, c:c + LANES]
                slabs.append(s if mask is None else jnp.where(mask, s, NEG))
            m_prev = m_scr[sl, :]
            m_new = jnp.maximum(m_prev, jnp.max(_tree(jnp.maximum, slabs), axis=1, keepdims=True))
            alpha = jnp.exp2(m_prev - m_new)
            ps = [jnp.exp2(x - m_new) for x in slabs]
            l_scr[sl, :] = alpha * l_scr[sl, :] + _tree(jnp.add, ps)
            m_scr[sl, :] = m_new
            acc_scr[sl, :] = alpha * acc_scr[sl, :]
            p_scr[slot, sl, :] = _pad_lanes([p.astype(BF16) for p in ps], tk, BF16)

    def weighted_values(j, slot, rows):
        off = pl.multiple_of(j * tk, tk)
        sl = slice(rows[0], rows[1])
        acc_scr[sl, :] += _dot(p_scr[slot, sl, :], v_ref[pl.ds(off, tk), :])

    def attend(j, slot, rows, diag_rows=None):
        softmax(slot, rows, diag_rows)
        weighted_values(j, slot, rows)

    scores(0, 0, all_rows)

    def body(jj, carry):
        a = 2 * jj
        attend(a, 0, all_rows)
        scores(a + 1, 1, all_rows)
        scores(a + 2, 0, all_rows)
        attend(a + 1, 1, all_rows)
        return carry

    lax.fori_loop(0, qi, body, 0)

    d = 2 * qi
    scores(d + 1, 1, hi_rows)
    attend(d, 0, all_rows, diag_rows=(0, tk))
    attend(d + 1, 1, hi_rows, diag_rows=hi_rows)

    o = acc_scr[...] / jnp.sum(l_scr[...], axis=1, keepdims=True)
    y_ref[...] = (o * g_ref[...].astype(F32)).astype(BF16)


def _fox_attention(q, k, v, g, c, *, batch, seq):
    nq = seq // ATT_TQ
    tok_spec = pl.BlockSpec((ATT_TQ, HEAD_DIM), lambda b, h, i: (b * nq + i, h))
    seq_spec = pl.BlockSpec((seq, HEAD_DIM), lambda b, h, i: (b, h))
    return pl.pallas_call(
        _fox_kernel, grid=(batch, N_HEADS, nq),
        in_specs=[tok_spec, seq_spec, seq_spec, tok_spec,
                  pl.BlockSpec((None, ATT_TQ, LANES), lambda b, h, i: (b, i, 0)),
                  pl.BlockSpec((None, seq, LANES), lambda b, h, i: (b, 0, 0))],
        out_specs=tok_spec,
        out_shape=jax.ShapeDtypeStruct((batch * seq, D_INNER), BF16),
        scratch_shapes=[pltpu.VMEM((seq, 2 * HEAD_DIM), BF16),
                        pltpu.VMEM((ATT_TQ, 2 * HEAD_DIM), BF16),
                        pltpu.VMEM((2, ATT_TQ, ATT_TK), F32), pltpu.VMEM((2, ATT_TQ, ATT_TK), BF16),
                        pltpu.VMEM((ATT_TQ, LANES), F32), pltpu.VMEM((ATT_TQ, LANES), F32),
                        pltpu.VMEM((ATT_TQ, HEAD_DIM), F32)],
        compiler_params=_params(3), name="fox_attention",
    )(q, k, v, g, c, c)


def _log_gates(z):
    lsig = jnp.minimum(z, 0.0) - jnp.log2(1.0 + jnp.exp2(-jnp.abs(z)))
    return lsig, lsig - z


def _sb_kernel(q_ref, k_ref, v_ref, g_ref, tri_ref, y_ref,
               z_scr, l_scr, cs_scr, u_scr, a_scr, rs_scr, r_scr, acc_scr):
    qi = pl.program_id(2)
    tq, tk = ATT_TQ, ATT_TK
    seg = SB_SEG
    n_seg = tk // seg
    all_rows, lo_rows, hi_rows = (0, tq), (0, tk), (tk, tq)

    r_scr[...] = jnp.zeros((tq, LANES), F32)
    acc_scr[...] = jnp.zeros((tq, HEAD_DIM), F32)

    def scores(j, k, rows):
        off = pl.multiple_of(j * tk, tk)
        z_scr[k, rows[0]:rows[1], :] = _dot_nt(q_ref[rows[0]:rows[1], :], k_ref[pl.ds(off, tk), :])

    def log_gates(k, rows, diag_rows=None):
        for r in range(rows[0], rows[1], ATT_RC):
            sl = slice(r, r + ATT_RC)
            live = _live_slabs(r, diag_rows, tk, strict=True)
            width = len(live) * LANES
            lsig, u = _log_gates(z_scr[k, sl, 0:width])
            us = {c: u[:, c:c + LANES] if mask is None else jnp.where(mask, u[:, c:c + LANES], 0.0)
                  for c, mask in live}
            l_scr[k, sl, 0:width] = lsig
            u_scr[k, sl, :] = _pad_lanes([us[c].astype(BF16) for c, _ in live], tk, BF16)
            for sgi in range(n_seg):
                slabs = [us[c] for c, _ in live if sgi * seg <= c < (sgi + 1) * seg]
                if slabs:
                    rs_scr[k, sgi, sl, :] = jnp.broadcast_to(
                        jnp.sum(_tree(jnp.add, slabs), axis=1, keepdims=True), (ATT_RC, LANES))
                else:
                    rs_scr[k, sgi, sl, :] = jnp.zeros((ATT_RC, LANES), F32)

    def suffix_sums(k, rows):
        sl = slice(rows[0], rows[1])
        for sgi in range(n_seg):
            cols = slice(sgi * seg, (sgi + 1) * seg)
            cs_scr[k, sl, cols] = _dot(u_scr[k, sl, cols], tri_ref[...])

    def weights(k, rows, diag_rows=None):
        for r in range(rows[0], rows[1], ATT_RC):
            sl = slice(r, r + ATT_RC)
            base = r_scr[sl, :]
            live = _live_slabs(r, diag_rows, tk, strict=True)
            width = len(live) * LANES
            x = l_scr[k, sl, 0:width] + cs_scr[k, sl, 0:width]
            parts = {}
            for sgi in reversed(range(n_seg)):
                for c, mask in live:
                    if sgi * seg <= c < (sgi + 1) * seg:
                        a = jnp.exp2(x[:, c:c + LANES] + base)
                        parts[c] = a if mask is None else jnp.where(mask, a, 0.0)
                base = base + rs_scr[k, sgi, sl, :]
            a_scr[k, sl, :] = _pad_lanes([parts[c].astype(BF16) for c, _ in live], tk, BF16)
            r_scr[sl, :] = base

    def weighted_values(j, k, rows):
        off = pl.multiple_of(j * tk, tk)
        sl = slice(rows[0], rows[1])
        acc_scr[sl, :] += _dot(a_scr[k, sl, :], v_ref[pl.ds(off, tk), :])

    d = 2 * qi

    def gates(k, rows, diag_rows=None):
        log_gates(k, rows, diag_rows)
        suffix_sums(k, rows)

    def finish(j, k, rows, diag_rows=None):
        weights(k, rows, diag_rows)
        weighted_values(j, k, rows)

    scores(d, 0, all_rows)
    scores(d + 1, 1, hi_rows)
    gates(0, all_rows, diag_rows=lo_rows)
    gates(1, hi_rows, diag_rows=hi_rows)

    def finish_and_prepare(finish_pair, ja_next):
        finish_pair()
        scores(ja_next, 1, all_rows)
        scores(ja_next - 1, 0, all_rows)
        gates(1, all_rows)
        gates(0, all_rows)

    def finish_diagonal():
        finish(d + 1, 1, hi_rows, diag_rows=hi_rows)
        finish(d, 0, all_rows, diag_rows=lo_rows)

    @pl.when(qi == 0)
    def _():
        finish_diagonal()

    @pl.when(qi > 0)
    def _():
        finish_and_prepare(finish_diagonal, d - 1)

    def body(jj, carry):
        ja = d - 1 - 2 * jj

        def finish_pair():
            finish(ja, 1, all_rows)
            finish(ja - 1, 0, all_rows)

        finish_and_prepare(finish_pair, ja - 2)
        return carry

    lax.fori_loop(0, qi - 1, body, 0)

    @pl.when(qi > 0)
    def _():
        finish(1, 1, all_rows)
        finish(0, 0, all_rows)

    y_ref[...] = (acc_scr[...] * g_ref[...].astype(F32)).astype(BF16)


def _sb_attention(q, k, v, g, tri_strict, *, batch, seq):
    nq = seq // ATT_TQ
    tok_spec = pl.BlockSpec((ATT_TQ, HEAD_DIM), lambda b, h, i: (b * nq + i, h))
    seq_spec = pl.BlockSpec((seq, HEAD_DIM), lambda b, h, i: (b, h))
    tile_f32 = pltpu.VMEM((2, ATT_TQ, ATT_TK), F32)
    tile_bf16 = pltpu.VMEM((2, ATT_TQ, ATT_TK), BF16)
    return pl.pallas_call(
        _sb_kernel, grid=(batch, N_HEADS, nq),
        in_specs=[tok_spec, seq_spec, seq_spec, tok_spec,
                  pl.BlockSpec((SB_SEG, SB_SEG), lambda b, h, i: (0, 0))],
        out_specs=tok_spec,
        out_shape=jax.ShapeDtypeStruct((batch * seq, D_INNER), BF16),
        scratch_shapes=[tile_f32, tile_f32, tile_f32, tile_bf16, tile_bf16,
                        pltpu.VMEM((2, ATT_TK // SB_SEG, ATT_TQ, LANES), F32),
                        pltpu.VMEM((ATT_TQ, LANES), F32), pltpu.VMEM((ATT_TQ, HEAD_DIM), F32)],
        compiler_params=_params(3), name="sb_attention",
    )(q, k, v, g, tri_strict)


def _heads_on_lanes(past_ref):
    x = past_ref[...]
    return x.reshape(x.shape[0], x.shape[1] * x.shape[2])


def _gather_keys(dst, past, new_ref, hh, past_len, n_new):
    sl = slice(hh * HEAD_DIM, (hh + 1) * HEAD_DIM)
    dst[0:past_len, :] = past[:, sl].astype(BF16)
    pad = jnp.zeros((SAMPLE_KPAD - n_new, HEAD_DIM), BF16)
    dst[past_len:past_len + SAMPLE_KPAD, :] = jnp.concatenate([new_ref[:, sl], pad], axis=0)


def _fox_sample_kernel(q_ref, kn_ref, vn_ref, g_ref, kp_ref, vp_ref, lft_ref, tri_ref, y_ref,
                       kb_scr, vb_scr, c_scr, *, past_len, n_new):
    hg = pl.program_id(1)
    kpad = past_len + SAMPLE_KPAD

    @pl.when(hg == 0)
    def _():
        tri = tri_ref[...]
        x1, x2, x3 = _split3(lft_ref[...])
        c_scr[...] = (_dot(x1, tri) + _dot(x2, tri)) + _dot(x3, tri)

    row = lax.broadcasted_iota(jnp.int32, (n_new, kpad), 0)
    col = lax.broadcasted_iota(jnp.int32, (n_new, kpad), 1)
    qpos = row + past_len
    k_past, v_past = _heads_on_lanes(kp_ref), _heads_on_lanes(vp_ref)
    for hh in range(HEAD_GROUP):
        sl = slice(hh * HEAD_DIM, (hh + 1) * HEAD_DIM)
        _gather_keys(kb_scr, k_past, kn_ref, hh, past_len, n_new)
        _gather_keys(vb_scr, v_past, vn_ref, hh, past_len, n_new)
        ck = c_scr[pl.ds(hg * HEAD_GROUP + hh, 1), :] * LOG2E
        cq = jnp.sum(jnp.where(col == qpos, ck, 0.0), axis=1, keepdims=True)
        s = _dot_nt(q_ref[:, sl], kb_scr[...]) + (cq - ck)
        s = jnp.where(col <= qpos, s, NEG)
        m = jnp.max(s, axis=1, keepdims=True)
        p = jnp.exp2(s - m)
        l = jnp.sum(p, axis=1, keepdims=True)
        o = _dot(p.astype(BF16), vb_scr[...]) / l
        y_ref[:, sl] = (o * g_ref[:, sl].astype(F32)).astype(BF16)


def _sb_sample_kernel(q_ref, kn_ref, vn_ref, g_ref, kp_ref, vp_ref, tri_ref, y_ref,
                      kb_scr, vb_scr, l_scr, u_scr, *, past_len, n_new):
    kpad = past_len + SAMPLE_KPAD
    row = lax.broadcasted_iota(jnp.int32, (n_new, kpad), 0)
    col = lax.broadcasted_iota(jnp.int32, (n_new, kpad), 1)
    mask = col < row + past_len
    k_past, v_past = _heads_on_lanes(kp_ref), _heads_on_lanes(vp_ref)
    for hh in range(HEAD_GROUP):
        sl = slice(hh * HEAD_DIM, (hh + 1) * HEAD_DIM)
        rows = slice(hh * n_new, (hh + 1) * n_new)
        _gather_keys(kb_scr, k_past, kn_ref, hh, past_len, n_new)
        _gather_keys(vb_scr.at[hh], v_past, vn_ref, hh, past_len, n_new)
        lsig, u = _log_gates(_dot_nt(q_ref[:, sl], kb_scr[...]))
        l_scr[rows, :] = lsig
        u_scr[rows, :] = jnp.where(mask, u, 0.0).astype(BF16)
    after = _dot(u_scr[...], tri_ref[...])
    for hh in range(HEAD_GROUP):
        sl = slice(hh * HEAD_DIM, (hh + 1) * HEAD_DIM)
        rows = slice(hh * n_new, (hh + 1) * n_new)
        a = jnp.where(mask, jnp.exp2(l_scr[rows, :] + after[rows, :]), 0.0)
        o = _dot(a.astype(BF16), vb_scr[hh])
        y_ref[:, sl] = (o * g_ref[:, sl].astype(F32)).astype(BF16)


def _sample_attention(kind, q, k_new, v_new, g, k_past, v_past, tri, lf_all_t=None):
    batch, past_len = k_past.shape[0], k_past.shape[1]
    n_new = q.shape[0] // batch
    kpad = past_len + SAMPLE_KPAD
    gw = HEAD_GROUP * HEAD_DIM
    tok_spec = pl.BlockSpec((n_new, gw), lambda b, hg: (b, hg))
    past_spec = pl.BlockSpec((None, past_len, HEAD_GROUP, HEAD_DIM), lambda b, hg: (b, 0, hg, 0))
    tri_spec = pl.BlockSpec((kpad, kpad), lambda b, hg: (0, 0))
    kp, vp = k_past, v_past
    if kind == "fox":
        body = partial(_fox_sample_kernel, past_len=past_len, n_new=n_new)
        in_specs = [tok_spec, tok_spec, tok_spec, tok_spec, past_spec, past_spec,
                    pl.BlockSpec((None, N_HEADS, kpad), lambda b, hg: (b, 0, 0)), tri_spec]
        args = (q, k_new, v_new, g, kp, vp, lf_all_t, tri)
        scratch = [pltpu.VMEM((kpad, HEAD_DIM), BF16), pltpu.VMEM((kpad, HEAD_DIM), BF16),
                   pltpu.VMEM((N_HEADS, kpad), F32)]
    else:
        body = partial(_sb_sample_kernel, past_len=past_len, n_new=n_new)
        in_specs = [tok_spec, tok_spec, tok_spec, tok_spec, past_spec, past_spec, tri_spec]
        args = (q, k_new, v_new, g, kp, vp, tri)
        scratch = [pltpu.VMEM((kpad, HEAD_DIM), BF16),
                   pltpu.VMEM((HEAD_GROUP, kpad, HEAD_DIM), BF16),
                   pltpu.VMEM((HEAD_GROUP * n_new, kpad), F32),
                   pltpu.VMEM((HEAD_GROUP * n_new, kpad), BF16)]
    return pl.pallas_call(
        body, grid=(batch, N_HEADS // HEAD_GROUP), in_specs=in_specs, out_specs=tok_spec,
        out_shape=jax.ShapeDtypeStruct((batch * n_new, D_INNER), BF16),
        scratch_shapes=scratch, compiler_params=_params(2), name=kind + "_sample_attention",
    )(*args)


def _tri(n, strict, lower):
    r = lax.broadcasted_iota(jnp.int32, (n, n), 0)
    c = lax.broadcasted_iota(jnp.int32, (n, n), 1)
    if lower:
        keep = (r > c) if strict else (r >= c)
    else:
        keep = (r < c) if strict else (r <= c)
    return keep.astype(BF16)


def kernel(x_prompt, x_sample, cache_fox_k, cache_fox_v, cache_fox_logf, cache_sb_k, cache_sb_v,
           norm_0, w_in_0, b_f_0, w_out_0, norm_1, w_in_1, w_out_1, norm_f):
    batch, seq, _ = x_prompt.shape
    dec_batch, dec_seq, _ = x_sample.shape
    past_len = cache_fox_k.shape[1]
    n_p, n_s = batch * seq, dec_batch * dec_seq
    kpad = past_len + SAMPLE_KPAD

    bf = jnp.pad(b_f_0, (0, LANES - N_HEADS)).reshape(1, LANES)
    xp = x_prompt.reshape(n_p, D_MODEL)
    xs = x_sample.reshape(n_s, D_MODEL)

    w0t = w_in_0.T
    qp, lfp, xnp, qs, lfs, xns = _proj(xp, xs, w0t, 0, "q", bf, norm_0, transposed=True)
    kp0, kbp, ks0, kbs = _proj(xnp, xns, w0t, 1, "kv", transposed=True)
    vp0, vbp, vs0, vbs = _proj(xnp, xns, w0t, 2, "kv", transposed=True)
    gp, gs = _proj(xnp, xns, w0t, 3, "g", transposed=True)
    lfp = lfp.reshape(batch, seq, LANES)
    cum = _cumsum(lfp, _tri(CUM_T, strict=False, lower=True))
    yp = _fox_attention(qp, kbp, vbp, gp, cum, batch=batch, seq=seq)
    lfs = lfs[:, :N_HEADS].reshape(dec_batch, dec_seq, N_HEADS)
    lf_all = jnp.concatenate([cache_fox_logf, lfs], axis=1)
    lf_all_t = jnp.pad(jnp.swapaxes(lf_all, 1, 2), ((0, 0), (0, 0), (0, kpad - past_len - dec_seq)))
    ys = _sample_attention("fox", qs, kbs, vbs, gs, cache_fox_k, cache_fox_v,
                           _tri(kpad, strict=False, lower=False), lf_all_t)
    xp1, xnp, xs1, xns = _out_proj(xp, xs, yp, ys, w_out_0, norm_1, final=False)

    qp, qs = _proj(xnp, xns, w_in_1, 0, "q")
    kp1, kbp, ks1, kbs = _proj(xnp, xns, w_in_1, 1, "kv")
    vp1, vbp, vs1, vbs = _proj(xnp, xns, w_in_1, 2, "kv")
    gp, gs = _proj(xnp, xns, w_in_1, 3, "g")
    yp = _sb_attention(qp, kbp, vbp, gp, _tri(SB_SEG, strict=True, lower=True), batch=batch, seq=seq)
    ys = _sample_attention("sb", qs, kbs, vbs, gs, cache_sb_k, cache_sb_v,
                           _tri(kpad, strict=True, lower=True))
    y_prompt, y_sample = _out_proj(xp1, xs1, yp, ys, w_out_1, norm_f, final=True)

    hp = (batch, seq, N_HEADS, HEAD_DIM)
    hs = (dec_batch, dec_seq, N_HEADS, HEAD_DIM)
    return (y_prompt.reshape(batch, seq, D_MODEL), y_sample.reshape(dec_batch, dec_seq, D_MODEL),
            kp0.reshape(hp), vp0.reshape(hp), lfp[:, :, :N_HEADS],
            kp1.reshape(hp), vp1.reshape(hp),
            ks0.reshape(hs), vs0.reshape(hs), lfs,
            ks1.reshape(hs), vs1.reshape(hs))
```
